```python
import math
import jax, jax.numpy as jnp
from jax import lax
import numpy as np

D_MODEL = 2048
BATCH = 8
SEQ = 4096
DEPTH = 1
DEC_BATCH = 32
DEC_SEQ = 16
PAST_LEN = 1024

CHUNK = 64
LEFT_CHUNKS = 8
BAND_ROWS = LEFT_CHUNKS * CHUNK
HEAD_DIM = 128
A_HEADS = 8
A_WIDTH = A_HEADS * HEAD_DIM
MAX_REL = 256
N_REL = 2 * MAX_REL + 1
R_WIDTH = 1024
R_BLOCKS = 16
R_BLOCK = R_WIDTH // R_BLOCKS
CONV_W = 4
LRU_C = 8.0
N_MEM = 256
M_HEADS = 4
M_WIDTH = M_HEADS * HEAD_DIM
D_FF = 5632
EPS = 1e-6
IN_COLS = 3 * A_WIDTH + 2 * R_WIDTH + 2 * D_MODEL
SPLITS = [A_WIDTH, 2 * A_WIDTH, 3 * A_WIDTH, 3 * A_WIDTH + R_WIDTH,
          3 * A_WIDTH + 2 * R_WIDTH, 3 * A_WIDTH + 2 * R_WIDTH + D_MODEL]

kernel_name = 'hybrid_chunk_attn_rglru_macaron_step'


def _rmsnorm(x, g):
    xf = x.astype(jnp.float32)
    y = xf * lax.rsqrt(jnp.mean(xf * xf, axis=-1, keepdims=True) + EPS)
    return (y * g.astype(jnp.float32)).astype(x.dtype)


def _swiglu(x, wg, wu, wd):
    return (jax.nn.silu(x @ wg) * (x @ wu)) @ wd


def _chunk_attn_prompt(q, k, v, rel_bias):
    B, S, H, Dh = q.shape
    nc = S // CHUNK
    band = BAND_ROWS + CHUNK
    pad = ((0, 0), (BAND_ROWS, 0), (0, 0), (0, 0))
    kp = jnp.pad(k, pad)
    vp = jnp.pad(v, pad)
    qc = q.reshape(B, nc, CHUNK, H, Dh).transpose(1, 0, 2, 3, 4)
    qi = jnp.arange(CHUNK)[:, None]
    kj = jnp.arange(band)[None, :]
    rel = jnp.clip(qi + BAND_ROWS - kj, -MAX_REL, MAX_REL) + MAX_REL
    bias = rel_bias.astype(jnp.float32)[:, rel]
    scale = Dh ** -0.5

    def one_chunk(args):
        c, qb = args
        start = c * CHUNK
        kb = lax.dynamic_slice_in_dim(kp, start, band, axis=1)
        vb = lax.dynamic_slice_in_dim(vp, start, band, axis=1)
        s = jnp.einsum('bqhd,bkhd->bhqk', qb, kb).astype(jnp.float32) * scale + bias[None]
        kpos = start - BAND_ROWS + jnp.arange(band)
        s = jnp.where((kpos >= 0)[None, None, None, :], s, -1e30)
        p = jax.nn.softmax(s, axis=-1).astype(vb.dtype)
        return jnp.einsum('bhqk,bkhd->bqhd', p, vb)

    out = lax.map(one_chunk, (jnp.arange(nc), qc))
    return out.transpose(1, 0, 2, 3, 4).reshape(B, S, H * Dh)


def _chunk_attn_sample(q, k, v, rel_bias, k_cache, v_cache):
    Bd, T, H, Dh = q.shape
    W = k_cache.shape[1]
    kk = jnp.concatenate([k_cache, k], axis=1)
    vv = jnp.concatenate([v_cache, v], axis=1)
    qpos = PAST_LEN + jnp.arange(T)
    kpos = jnp.concatenate([PAST_LEN - W + jnp.arange(W), PAST_LEN + jnp.arange(T)])
    rel = jnp.clip(qpos[:, None] - kpos[None, :], -MAX_REL, MAX_REL) + MAX_REL
    bias = rel_bias.astype(jnp.float32)[:, rel]
    qc = qpos[:, None] // CHUNK
    kc = kpos[None, :] // CHUNK
    vis = (kc <= qc) & (kc >= qc - LEFT_CHUNKS) & (kpos[None, :] >= 0)
    s = jnp.einsum('bqhd,bkhd->bhqk', q, kk).astype(jnp.float32) * (Dh ** -0.5) + bias[None]
    s = jnp.where(vis[None, None], s, -1e30)
    p = jax.nn.softmax(s, axis=-1).astype(vv.dtype)
    return jnp.einsum('bhqk,bkhd->bqhd', p, vv).reshape(Bd, T, H * Dh)


def _causal_conv(xr, buf, w, b):
    T = xr.shape[1]
    xp = jnp.concatenate([buf.astype(xr.dtype), xr], axis=1)
    y = b
    for j in range(CONV_W):
        y = y + xp[:, j:j + T] * w[j]
    return y, xp[:, -(CONV_W - 1):]


def _rglru(xc, h0, wa, ba, wx, bx, lam):
    B, T, _ = xc.shape
    xb = xc.reshape(B, T, R_BLOCKS, R_BLOCK)
    r = jax.nn.sigmoid((jnp.einsum('btni,nij->btnj', xb, wa) + ba).astype(jnp.float32)).reshape(B, T, R_WIDTH)
    ig = jax.nn.sigmoid((jnp.einsum('btni,nij->btnj', xb, wx) + bx).astype(jnp.float32)).reshape(B, T, R_WIDTH)
    log_a = -LRU_C * r * jax.nn.softplus(-lam.astype(jnp.float32))
    a = jnp.exp(log_a)
    mult = jnp.sqrt(-jnp.expm1(2.0 * log_a))
    bterm = mult * (ig * xc.astype(jnp.float32))
    bterm = bterm.at[:, 0].add(a[:, 0] * h0.astype(jnp.float32))

    def comb(l, rr):
        return (l[0] * rr[0], rr[0] * l[1] + rr[1])

    _, h = lax.associative_scan(comb, (a, bterm), axis=1)
    return h.astype(xc.dtype), h[:, -1].astype(xc.dtype)


def _mem_kv(mem, g_m, w_mk, w_mv, mk_norm):
    B, N, _ = mem.shape
    m = _rmsnorm(mem, g_m)
    k = _rmsnorm((m @ w_mk).reshape(B, N, M_HEADS, HEAD_DIM), mk_norm)
    v = (m @ w_mv).reshape(B, N, M_HEADS, HEAD_DIM)
    return k, v


def _mem_attend(q, k, v):
    B, T = q.shape[:2]
    s = jnp.einsum('bthd,bnhd->bhtn', q, k).astype(jnp.float32) * (HEAD_DIM ** -0.5)
    p = jax.nn.softmax(s, axis=-1).astype(v.dtype)
    return jnp.einsum('bhtn,bnhd->bthd', p, v).reshape(B, T, M_WIDTH)


def _layer(x, lp, attend, conv_buf, h0, mem_k, mem_v):
    B, T, _ = x.shape
    x = x + 0.5 * _swiglu(_rmsnorm(x, lp['ffn1_norm']), lp['ffn1_wg'], lp['ffn1_wu'], lp['ffn1_wd'])
    u = _rmsnorm(x, lp['mix_norm']) @ lp['w_in']
    q, k, v, xr, gr, ga, gb = jnp.split(u, SPLITS, axis=-1)
    q = _rmsnorm(q.reshape(B, T, A_HEADS, HEAD_DIM), lp['q_norm'])
    k = _rmsnorm(k.reshape(B, T, A_HEADS, HEAD_DIM), lp['k_norm'])
    v = v.reshape(B, T, A_HEADS, HEAD_DIM)
    a_out = attend(q, k, v, lp['rel_bias'])
    xc, new_buf = _causal_conv(xr, conv_buf, lp['conv_w'], lp['conv_b'])
    h, h_last = _rglru(xc, h0, lp['lru_wa'], lp['lru_ba'], lp['lru_wx'], lp['lru_bx'], lp['lru_lambda'])
    r_out = jax.nn.gelu(gr) * h
    merged = jax.nn.sigmoid(ga) * (a_out @ lp['w_proj_a']) + jax.nn.sigmoid(gb) * (r_out @ lp['w_proj_b'])
    x = x + merged @ lp['w_out']
    mq = _rmsnorm((_rmsnorm(x, lp['mem_norm_x']) @ lp['w_mq']).reshape(B, T, M_HEADS, HEAD_DIM), lp['mq_norm'])
    x = x + _mem_attend(mq, mem_k, mem_v) @ lp['w_mo']
    x = x + 0.5 * _swiglu(_rmsnorm(x, lp['ffn2_norm']), lp['ffn2_wg'], lp['ffn2_wu'], lp['ffn2_wd'])
    x = _rmsnorm(x, lp['final_norm'])
    return x, k, v, new_buf, h_last


def setup_inputs(seed: int = 0) -> dict:
    key = jax.random.key(seed)
    ks = iter(jax.random.split(key, 64))
    f32 = jnp.float32

    def nrm(shape, scale):
        return jax.random.normal(next(ks), shape, f32) * scale

    def gain(n):
        return 1.0 + 0.05 * jax.random.normal(next(ks), (DEPTH, n), f32)

    wc = min(BAND_ROWS, PAST_LEN)
    inp = {}
    inp['x_prompt'] = nrm((BATCH, SEQ, D_MODEL), 1.0)
    inp['x_sample'] = nrm((DEC_BATCH, DEC_SEQ, D_MODEL), 1.0)
    inp['cache_attn_k'] = nrm((DEPTH, DEC_BATCH, wc, A_HEADS, HEAD_DIM), 1.0)
    inp['cache_attn_v'] = nrm((DEPTH, DEC_BATCH, wc, A_HEADS, HEAD_DIM), 1.0)
    inp['state_conv'] = nrm((DEPTH, DEC_BATCH, CONV_W - 1, R_WIDTH), 1.0)
    inp['state_lru'] = nrm((DEPTH, DEC_BATCH, R_WIDTH), 0.5)
    inp['cache_mem_k'] = nrm((DEPTH, DEC_BATCH, N_MEM, M_HEADS, HEAD_DIM), 1.0)
    inp['cache_mem_v'] = nrm((DEPTH, DEC_BATCH, N_MEM, M_HEADS, HEAD_DIM), 1.0)
    inp['mem_prompt'] = nrm((BATCH, N_MEM, D_MODEL), 1.0)
    inp['ffn1_norm'] = gain(D_MODEL)
    inp['ffn1_wg'] = nrm((DEPTH, D_MODEL, D_FF), D_MODEL ** -0.5)
    inp['ffn1_wu'] = nrm((DEPTH, D_MODEL, D_FF), D_MODEL ** -0.5)
    inp['ffn1_wd'] = nrm((DEPTH, D_FF, D_MODEL), D_FF ** -0.5)
    inp['mix_norm'] = gain(D_MODEL)
    inp['w_in'] = nrm((DEPTH, D_MODEL, IN_COLS), D_MODEL ** -0.5)
    inp['q_norm'] = gain(HEAD_DIM)
    inp['k_norm'] = gain(HEAD_DIM)
    inp['rel_bias'] = nrm((DEPTH, A_HEADS, N_REL), 0.1)
    inp['conv_w'] = nrm((DEPTH, CONV_W, R_WIDTH), 0.5)
    inp['conv_b'] = nrm((DEPTH, R_WIDTH), 0.02)
    inp['lru_wa'] = nrm((DEPTH, R_BLOCKS, R_BLOCK, R_BLOCK), R_BLOCK ** -0.5)
    inp['lru_ba'] = nrm((DEPTH, R_BLOCKS, R_BLOCK), 0.02)
    inp['lru_wx'] = nrm((DEPTH, R_BLOCKS, R_BLOCK, R_BLOCK), R_BLOCK ** -0.5)
    inp['lru_bx'] = nrm((DEPTH, R_BLOCKS, R_BLOCK), 0.02)
    a0 = jax.random.uniform(next(ks), (DEPTH, R_WIDTH), f32, 0.9, 0.999)
    inp['lru_lambda'] = jnp.log(a0) - jnp.log1p(-a0)
    inp['w_proj_a'] = nrm((DEPTH, A_WIDTH, D_MODEL), A_WIDTH ** -0.5)
    inp['w_proj_b'] = nrm((DEPTH, R_WIDTH, D_MODEL), R_WIDTH ** -0.5)
    inp['w_out'] = nrm((DEPTH, D_MODEL, D_MODEL), D_MODEL ** -0.5)
    inp['mem_norm_x'] = gain(D_MODEL)
    inp['mem_norm_m'] = gain(D_MODEL)
    inp['w_mq'] = nrm((DEPTH, D_MODEL, M_WIDTH), D_MODEL ** -0.5)
    inp['w_mk'] = nrm((DEPTH, D_MODEL, M_WIDTH), D_MODEL ** -0.5)
    inp['w_mv'] = nrm((DEPTH, D_MODEL, M_WIDTH), D_MODEL ** -0.5)
    inp['mq_norm'] = gain(HEAD_DIM)
    inp['mk_norm'] = gain(HEAD_DIM)
    inp['w_mo'] = nrm((DEPTH, M_WIDTH, D_MODEL), M_WIDTH ** -0.5)
    inp['ffn2_norm'] = gain(D_MODEL)
    inp['ffn2_wg'] = nrm((DEPTH, D_MODEL, D_FF), D_MODEL ** -0.5)
    inp['ffn2_wu'] = nrm((DEPTH, D_MODEL, D_FF), D_MODEL ** -0.5)
    inp['ffn2_wd'] = nrm((DEPTH, D_FF, D_MODEL), D_FF ** -0.5)
    inp['final_norm'] = gain(D_MODEL)
    return inp


def reference(x_prompt, x_sample, cache_attn_k, cache_attn_v, state_conv, state_lru,
              cache_mem_k, cache_mem_v, mem_prompt,
              ffn1_norm, ffn1_wg, ffn1_wu, ffn1_wd, mix_norm, w_in, q_norm, k_norm, rel_bias,
              conv_w, conv_b, lru_wa, lru_ba, lru_wx, lru_bx, lru_lambda,
              w_proj_a, w_proj_b, w_out, mem_norm_x, mem_norm_m, w_mq, w_mk, w_mv,
              mq_norm, mk_norm, w_mo, ffn2_norm, ffn2_wg, ffn2_wu, ffn2_wd, final_norm):
    yp, ys = x_prompt, x_sample
    pk, pv, pconv, plru, pmk, pmv = [], [], [], [], [], []
    sk, sv, sconv, slru = [], [], [], []
    B, S, _ = x_prompt.shape
    Bd = x_sample.shape[0]
    n_keep = min(BAND_ROWS, S)
    for l in range(DEPTH):
        lp = {'ffn1_norm': ffn1_norm[l], 'ffn1_wg': ffn1_wg[l], 'ffn1_wu': ffn1_wu[l], 'ffn1_wd': ffn1_wd[l],
              'mix_norm': mix_norm[l], 'w_in': w_in[l], 'q_norm': q_norm[l], 'k_norm': k_norm[l],
              'rel_bias': rel_bias[l], 'conv_w': conv_w[l], 'conv_b': conv_b[l],
              'lru_wa': lru_wa[l], 'lru_ba': lru_ba[l], 'lru_wx': lru_wx[l], 'lru_bx': lru_bx[l],
              'lru_lambda': lru_lambda[l], 'w_proj_a': w_proj_a[l], 'w_proj_b': w_proj_b[l],
              'w_out': w_out[l], 'mem_norm_x': mem_norm_x[l], 'w_mq': w_mq[l], 'mq_norm': mq_norm[l],
              'w_mo': w_mo[l], 'ffn2_norm': ffn2_norm[l], 'ffn2_wg': ffn2_wg[l], 'ffn2_wu': ffn2_wu[l],
              'ffn2_wd': ffn2_wd[l], 'final_norm': final_norm[l]}
        mk, mv = _mem_kv(mem_prompt, mem_norm_m[l], w_mk[l], w_mv[l], mk_norm[l])
        buf0 = jnp.zeros((B, CONV_W - 1, R_WIDTH), yp.dtype)
        h0 = jnp.zeros((B, R_WIDTH), jnp.float32)
        yp, k_p, v_p, cb_p, hl_p = _layer(yp, lp, _chunk_attn_prompt, buf0, h0, mk, mv)
        pk.append(k_p[:, S - n_keep:])
        pv.append(v_p[:, S - n_keep:])
        pconv.append(cb_p)
        plru.append(hl_p)
        pmk.append(mk)
        pmv.append(mv)
        kc_l, vc_l = cache_attn_k[l], cache_attn_v[l]
        attend_s = lambda q, k, v, rb, kc=kc_l, vc=vc_l: _chunk_attn_sample(q, k, v, rb, kc, vc)
        ys, k_s, v_s, cb_s, hl_s = _layer(ys, lp, attend_s, state_conv[l], state_lru[l],
                                          cache_mem_k[l], cache_mem_v[l])
        sk.append(k_s)
        sv.append(v_s)
        sconv.append(cb_s)
        slru.append(hl_s)
    k_prompt = jnp.stack(pk)
    v_prompt = jnp.stack(pv)
    conv_prompt = jnp.stack(pconv)
    lru_prompt = jnp.stack(plru)
    mem_k_prompt = jnp.stack(pmk)
    mem_v_prompt = jnp.stack(pmv)
    k_sample = jnp.stack(sk)
    v_sample = jnp.stack(sv)
    conv_sample = jnp.stack(sconv)
    lru_sample = jnp.stack(slru)
    return (yp, ys, k_prompt, v_prompt, conv_prompt, lru_prompt, mem_k_prompt, mem_v_prompt,
            k_sample, v_sample, conv_sample, lru_sample)
```

```python
import functools

import numpy as np
import jax
import jax.numpy as jnp
from jax import lax
from jax.experimental import pallas as pl
from jax.experimental.pallas import tpu as pltpu

F32 = jnp.float32
BF16 = jnp.bfloat16

CHUNK = 64
LEFT_CHUNKS = 8
BAND_ROWS = LEFT_CHUNKS * CHUNK
HEAD_DIM = 128
PAST_LEN = 1024
CONV_W = 4
LRU_C = 8.0
EPS = 1e-6
NEG_INF = -1e30

VMEM_LIMIT_BYTES = 56 * 1024 * 1024
LANES = 128

ROW_TILE = 512
FFN_COL_TILE = 512
MERGE_COL_TILE = 512
ATTN_Q_TILE = 256
LRU_TIME_TILE = 64
MEM_ROW_TILE = 128


def _cparams(*sem):
    return pltpu.CompilerParams(dimension_semantics=sem, vmem_limit_bytes=VMEM_LIMIT_BYTES)


def _rms(x, g):
    ms = jnp.mean(x * x, axis=-1, keepdims=True)
    return x * lax.rsqrt(ms + EPS) * g


def _store_head_rms(u, g, o_ref):
    for h in range(u.shape[-1] // HEAD_DIM):
        sl = slice(h * HEAD_DIM, (h + 1) * HEAD_DIM)
        o_ref[:, sl] = _rms(u[:, sl], g).astype(o_ref.dtype)


def _dot(a, b):
    return jnp.dot(a, b, preferred_element_type=F32)


def _dot_nt(a, b):
    return lax.dot_general(a, b, (((1,), (1,)), ((), ())), preferred_element_type=F32)


def _ffn_kernel(*refs, nf, final_norm):
    if final_norm:
        x_ref, gn_ref, wg_ref, wu_ref, wd_ref, gf_ref, o_ref, xn_ref, acc_ref = refs
    else:
        x_ref, gn_ref, wg_ref, wu_ref, wd_ref, o_ref, xn_ref, acc_ref = refs
    f = pl.program_id(1)

    @pl.when(f == 0)
    def _():
        xn_ref[...] = _rms(x_ref[...], gn_ref[...]).astype(BF16)
        acc_ref[...] = jnp.zeros_like(acc_ref)

    xn = xn_ref[...]
    g = _dot(xn, wg_ref[...])
    u = _dot(xn, wu_ref[...])
    h = (jax.nn.silu(g) * u).astype(BF16)
    acc_ref[...] += _dot(h, wd_ref[...])

    @pl.when(f == nf - 1)
    def _():
        y = x_ref[...] + 0.5 * acc_ref[...]
        if final_norm:
            y = _rms(y, gf_ref[...])
        o_ref[...] = y


def _ffn(x, gn, wg, wu, wd, final_gain=None):
    n, d = x.shape
    dff = wg.shape[1]
    tm, tf = min(ROW_TILE, n), FFN_COL_TILE
    nf = dff // tf
    final_norm = final_gain is not None
    row = pl.BlockSpec((tm, d), lambda i, f: (i, 0))
    vec = pl.BlockSpec((1, d), lambda i, f: (0, 0))
    in_specs = [row, vec,
                pl.BlockSpec((d, tf), lambda i, f: (0, f)),
                pl.BlockSpec((d, tf), lambda i, f: (0, f)),
                pl.BlockSpec((tf, d), lambda i, f: (f, 0))]
    args = [x, gn, wg, wu, wd]
    if final_norm:
        in_specs.append(vec)
        args.append(final_gain)
    return pl.pallas_call(
        functools.partial(_ffn_kernel, nf=nf, final_norm=final_norm),
        grid=(n // tm, nf),
        in_specs=in_specs,
        out_specs=row,
        out_shape=jax.ShapeDtypeStruct((n, d), F32),
        scratch_shapes=[pltpu.VMEM((tm, d), BF16), pltpu.VMEM((tm, d), F32)],
        compiler_params=_cparams("parallel", "arbitrary"),
        name="ffn",
    )(*args)


def _inproj_kernel(x_ref, gn_ref, w_ref, qg_ref, kg_ref,
                   q_ref, k_ref, v_ref, xr_ref, gr_ref, xn_ref):
    j = pl.program_id(1)

    @pl.when(j == 0)
    def _():
        xn_ref[...] = _rms(x_ref[...], gn_ref[...]).astype(BF16)

    u = _dot(xn_ref[...], w_ref[...])

    @pl.when(j == 0)
    def _():
        _store_head_rms(u, qg_ref[...], q_ref)

    @pl.when(j == 1)
    def _():
        _store_head_rms(u, kg_ref[...], k_ref)

    @pl.when(j == 2)
    def _():
        v_ref[...] = u

    @pl.when(j == 3)
    def _():
        xr_ref[...] = u

    @pl.when(j == 4)
    def _():
        gr_ref[...] = u


def _inproj(x, gn, w_in, qg, kg, *, w, batch, time_major):
    n, d = x.shape
    tm = min(ROW_TILE, n)
    seq = n // batch
    nt = seq // tm if time_major else 1
    row = pl.BlockSpec((tm, d), lambda i, j: (i, 0))
    out = pl.BlockSpec((tm, w), lambda i, j: (i, 0))
    if time_major:
        out_t = pl.BlockSpec((tm, w), lambda i, j: (i % nt, i // nt))
        t_shape = jax.ShapeDtypeStruct((seq, batch * w), F32)
    else:
        out_t, t_shape = out, jax.ShapeDtypeStruct((n, w), F32)
    vec = lambda m: pl.BlockSpec((1, m), lambda i, j: (0, 0))
    return pl.pallas_call(
        _inproj_kernel,
        grid=(n // tm, 5),
        in_specs=[row, vec(d), pl.BlockSpec((d, w), lambda i, j: (0, j)),
                  vec(HEAD_DIM), vec(HEAD_DIM)],
        out_specs=[out, out, out, out_t, out_t],
        out_shape=[jax.ShapeDtypeStruct((n, w), BF16),
                   jax.ShapeDtypeStruct((n, w), F32),
                   jax.ShapeDtypeStruct((n, w), F32),
                   t_shape, t_shape],
        scratch_shapes=[pltpu.VMEM((tm, d), BF16)],
        compiler_params=_cparams("parallel", "arbitrary"),
        name="inproj",
    )(x, gn, w_in, qg, kg)


def _memkv_kernel(m_ref, gn_ref, wk_ref, wv_ref, kg_ref, k_ref, v_ref):
    xn = _rms(m_ref[...], gn_ref[...]).astype(BF16)
    _store_head_rms(_dot(xn, wk_ref[...]), kg_ref[...], k_ref)
    v_ref[...] = _dot(xn, wv_ref[...])


def _memkv(m, gn, wk, wv, kg):
    n, d = m.shape
    mw = wk.shape[1]
    tm = min(ROW_TILE, n)
    row = lambda c: pl.BlockSpec((tm, c), lambda i: (i, 0))
    full = lambda a: pl.BlockSpec(a.shape, lambda i: (0, 0))
    return pl.pallas_call(
        _memkv_kernel,
        grid=(n // tm,),
        in_specs=[row(d), full(gn), full(wk), full(wv), full(kg)],
        out_specs=[row(mw), row(mw)],
        out_shape=[jax.ShapeDtypeStruct((n, mw), F32)] * 2,
        compiler_params=_cparams("parallel"),
        name="memkv",
    )(m, gn, wk, wv, kg)


def _band_attn_kernel(q_ref, k_ref, v_ref, bias_ref, o_ref, *, tq, win, scale):
    i = pl.program_id(2)
    nvar = bias_ref.shape[1]
    ws = pl.multiple_of(jnp.maximum(i * tq - BAND_ROWS, 0), tq)
    var = jnp.minimum(i, nvar - 1)
    k = k_ref[pl.ds(ws, win), :].astype(BF16)
    v = v_ref[pl.ds(ws, win), :].astype(BF16)
    s = _dot_nt(q_ref[...], k) * scale + bias_ref[0, var]
    m = jnp.max(s, axis=-1, keepdims=True)
    p = jnp.exp(s - m)
    l = jnp.sum(p, axis=-1, keepdims=True)
    o = _dot(p.astype(BF16), v)
    o_ref[...] = (o / l).astype(o_ref.dtype)


def _band_bias_tables(rel_bias, tq):
    max_rel = (rel_bias.shape[-1] - 1) // 2
    win = BAND_ROWS + tq
    nvar = BAND_ROWS // tq + 1
    qpos = np.arange(nvar)[:, None, None] * tq + np.arange(tq)[None, :, None]
    kpos = np.arange(win)[None, None, :]
    idx = np.clip(qpos - kpos, -max_rel, max_rel) + max_rel
    qc, kc = qpos // CHUNK, kpos // CHUNK
    vis = (kc <= qc) & (kc >= qc - LEFT_CHUNKS)
    return jnp.where(vis[None], rel_bias.astype(F32)[:, idx], NEG_INF)


def _band_attn(q, k, v, rel_bias, *, batch):
    n, w = q.shape
    seq = n // batch
    heads = w // HEAD_DIM
    tq = ATTN_Q_TILE
    win = BAND_ROWS + tq
    nt = seq // tq
    bias = _band_bias_tables(rel_bias, tq)
    qo = pl.BlockSpec((tq, HEAD_DIM), lambda h, b, i: (b * nt + i, h))
    kv = pl.BlockSpec((seq, HEAD_DIM), lambda h, b, i: (b, h))
    return pl.pallas_call(
        functools.partial(_band_attn_kernel, tq=tq, win=win, scale=HEAD_DIM ** -0.5),
        grid=(heads, batch, nt),
        in_specs=[qo, kv, kv,
                  pl.BlockSpec((1,) + bias.shape[1:], lambda h, b, i: (h, 0, 0, 0))],
        out_specs=qo,
        out_shape=jax.ShapeDtypeStruct((n, w), BF16),
        compiler_params=_cparams("parallel", "parallel", "arbitrary"),
        name="band_attn",
    )(q, k, v, bias)


def _step_attn_kernel(q_ref, kn_ref, vn_ref, kc_ref, vc_ref, bc_ref, bn_ref, o_ref, *, scale):
    for h in range(q_ref.shape[-1] // HEAD_DIM):
        sl = slice(h * HEAD_DIM, (h + 1) * HEAD_DIM)
        q = q_ref[:, sl]
        sc = _dot_nt(q, kc_ref[0, :, sl].astype(BF16)) * scale + bc_ref[h]
        sn = _dot_nt(q, kn_ref[:, sl].astype(BF16)) * scale + bn_ref[h]
        m = jnp.maximum(jnp.max(sc, axis=-1, keepdims=True), jnp.max(sn, axis=-1, keepdims=True))
        pc = jnp.exp(sc - m)
        pn = jnp.exp(sn - m)
        l = jnp.sum(pc, axis=-1, keepdims=True) + jnp.sum(pn, axis=-1, keepdims=True)
        o = _dot(pc.astype(BF16), vc_ref[0, :, sl].astype(BF16)) + _dot(pn.astype(BF16), vn_ref[:, sl].astype(BF16))
        o_ref[:, sl] = (o / l).astype(o_ref.dtype)


def _step_attn(q, k, v, rel_bias, k_cache, v_cache):
    bd, wc, w = k_cache.shape
    t = q.shape[0] // bd
    max_rel = (rel_bias.shape[-1] - 1) // 2
    qpos = PAST_LEN + np.arange(t)
    kpos = np.concatenate([PAST_LEN - wc + np.arange(wc), PAST_LEN + np.arange(t)])
    idx = np.clip(qpos[:, None] - kpos[None, :], -max_rel, max_rel) + max_rel
    qc, kc = qpos[:, None] // CHUNK, kpos[None, :] // CHUNK
    vis = (kc <= qc) & (kc >= qc - LEFT_CHUNKS) & (kpos[None, :] >= 0)
    bias = jnp.where(vis[None], rel_bias.astype(F32)[:, idx], NEG_INF)
    new = pl.BlockSpec((t, w), lambda b: (b, 0))
    cache = pl.BlockSpec((1, wc, w), lambda b: (b, 0, 0))
    full = lambda a: pl.BlockSpec(a.shape, lambda b: (0,) * a.ndim)
    bias_c, bias_n = bias[:, :, :wc], bias[:, :, wc:]
    return pl.pallas_call(
        functools.partial(_step_attn_kernel, scale=HEAD_DIM ** -0.5),
        grid=(bd,),
        in_specs=[new, new, new, cache, cache, full(bias_c), full(bias_n)],
        out_specs=new,
        out_shape=jax.ShapeDtypeStruct(q.shape, BF16),
        compiler_params=_cparams("parallel"),
        name="step_attn",
    )(q, k, v, k_cache, v_cache, bias_c, bias_n)


def _lru_kernel(xr_ref, gr_ref, buf0_ref, h0_ref, cw_ref, cb_ref, wgate_ref, ba_ref, bx_ref,
                lam_ref, ro_ref, hl_ref, tail_ref, xp_s, a_s, b_s, h_s, *, tt, nb):
    rows = tt * nb
    halo = (CONV_W - 1) * nb

    @pl.when(pl.program_id(0) == 0)
    def _():
        xp_s[0:halo, :] = buf0_ref[...]
        h_s[...] = h0_ref[...]

    xp_s[halo:halo + rows, :] = xr_ref[...]
    xc = cb_ref[...]
    for j in range(CONV_W):
        xc = xc + xp_s[j * nb:j * nb + rows, :] * cw_ref[j:j + 1, :]
    new_tail = xp_s[rows:rows + halo, :]
    xp_s[0:halo, :] = new_tail
    tail_ref[...] = new_tail

    sp = jax.nn.softplus(-lam_ref[...])
    xcb = xc.astype(BF16)
    for p in range(xc.shape[-1] // LANES):
        sl = slice(p * LANES, (p + 1) * LANES)
        z = _dot(xcb[:, sl], wgate_ref[p])
        r = jax.nn.sigmoid(z[:, :LANES] + ba_ref[:, sl])
        ig = jax.nn.sigmoid(z[:, LANES:] + bx_ref[:, sl])
        log_a = -LRU_C * r * sp[:, sl]
        a = jnp.exp(log_a)
        a_s[:, sl] = a
        b_s[:, sl] = jnp.sqrt(-jnp.tanh(log_a) * (a * a + 1.0)) * (ig * xc[:, sl])

    def step(t, h):
        r0 = pl.multiple_of(t * nb, nb)
        h = a_s[pl.ds(r0, nb), :] * h + b_s[pl.ds(r0, nb), :]
        b_s[pl.ds(r0, nb), :] = h
        return h

    h = lax.fori_loop(0, tt, step, h_s[...])
    h_s[...] = h
    hl_ref[...] = h
    ro_ref[...] = (jax.nn.gelu(gr_ref[...]) * b_s[...]).astype(ro_ref.dtype)


def _lru(xr, gr, buf0, h0, conv_w, conv_b, wgate, ba, bx, lam, *, nb):
    n, r = xr.shape
    t = n // nb
    tt = min(LRU_TIME_TILE, t)
    rows = tt * nb
    halo = (CONV_W - 1) * nb
    blk = pl.BlockSpec((rows, r), lambda i: (i, 0))
    full = lambda a: pl.BlockSpec(a.shape, lambda i: (0,) * a.ndim)
    return pl.pallas_call(
        functools.partial(_lru_kernel, tt=tt, nb=nb),
        grid=(t // tt,),
        in_specs=[blk, blk, full(buf0), full(h0), full(conv_w), full(conv_b), full(wgate),
                  full(ba), full(bx), full(lam)],
        out_specs=[blk, full(h0), full(buf0)],
        out_shape=[jax.ShapeDtypeStruct((n, r), BF16),
                   jax.ShapeDtypeStruct(h0.shape, F32),
                   jax.ShapeDtypeStruct(buf0.shape, F32)],
        scratch_shapes=[pltpu.VMEM((rows + halo, r), F32), pltpu.VMEM((rows, r), F32),
                        pltpu.VMEM((rows, r), F32), pltpu.VMEM((nb, r), F32)],
        compiler_params=_cparams("arbitrary"),
        name="conv_lru",
    )(xr, gr, buf0, h0, conv_w, conv_b, wgate, ba, bx, lam)


def _pair_gate_weights(wa, wx):
    nblk, bs, _ = wa.shape
    def pair(w):
        w = w.reshape(nblk // 2, 2, bs, bs)
        z = jnp.zeros_like(w[:, 0])
        top = jnp.concatenate([w[:, 0], z], axis=-1)
        bot = jnp.concatenate([z, w[:, 1]], axis=-1)
        return jnp.concatenate([top, bot], axis=-2)
    return jnp.concatenate([pair(wa), pair(wx)], axis=-1).astype(BF16)


def _merge_kernel(x_ref, gn_ref, a_ref, r_ref, wga_ref, wgb_ref, wpa_ref, wpb_ref, wo_ref,
                  o_ref, xn_ref, acc_ref, *, nc):
    c = pl.program_id(1)

    @pl.when(c == 0)
    def _():
        xn_ref[...] = _rms(x_ref[...], gn_ref[...]).astype(BF16)
        acc_ref[...] = jnp.zeros_like(acc_ref)

    xn = xn_ref[...]
    m = (jax.nn.sigmoid(_dot(xn, wga_ref[...])) * _dot(a_ref[...], wpa_ref[...])
         + jax.nn.sigmoid(_dot(xn, wgb_ref[...])) * _dot(r_ref[...], wpb_ref[...]))
    acc_ref[...] += _dot(m.astype(BF16), wo_ref[...])

    @pl.when(c == nc - 1)
    def _():
        o_ref[...] = x_ref[...] + acc_ref[...]


def _merge(x, gn, a_out, r_out, w_in, w_pa, w_pb, w_out, *, batch, r_time_major):
    n, d = x.shape
    w = a_out.shape[1]
    tm, tc = min(ROW_TILE, n), MERGE_COL_TILE
    nc = d // tc
    ga0 = (w_in.shape[1] - 2 * d) // tc
    gb0 = ga0 + nc
    nt = (n // batch) // tm if r_time_major else 1
    row = pl.BlockSpec((tm, d), lambda i, c: (i, 0))
    a_spec = pl.BlockSpec((tm, w), lambda i, c: (i, 0))
    r_spec = pl.BlockSpec((tm, w), lambda i, c: (i % nt, i // nt)) if r_time_major else a_spec
    return pl.pallas_call(
        functools.partial(_merge_kernel, nc=nc),
        grid=(n // tm, nc),
        in_specs=[row, pl.BlockSpec((1, d), lambda i, c: (0, 0)), a_spec, r_spec,
                  pl.BlockSpec((d, tc), lambda i, c: (0, ga0 + c)),
                  pl.BlockSpec((d, tc), lambda i, c: (0, gb0 + c)),
                  pl.BlockSpec((w, tc), lambda i, c: (0, c)),
                  pl.BlockSpec((w, tc), lambda i, c: (0, c)),
                  pl.BlockSpec((tc, d), lambda i, c: (c, 0))],
        out_specs=row,
        out_shape=jax.ShapeDtypeStruct((n, d), F32),
        scratch_shapes=[pltpu.VMEM((tm, d), BF16), pltpu.VMEM((tm, d), F32)],
        compiler_params=_cparams("parallel", "arbitrary"),
        name="merge",
    )(x, gn, a_out, r_out, w_in, w_in, w_pa, w_pb, w_out)


def _memattn_kernel(x_ref, gn_ref, wq_ref, qg_ref, mk_ref, mv_ref, wo_ref, o_ref, mq_s, att_s,
                    *, nb, tb, scale):
    x = x_ref[...]
    _store_head_rms(_dot(_rms(x, gn_ref[...]).astype(BF16), wq_ref[...]), qg_ref[...], mq_s)
    for h in range(mq_s.shape[-1] // HEAD_DIM):
        sl = slice(h * HEAD_DIM, (h + 1) * HEAD_DIM)
        for jb in range(nb):
            rs = slice(jb * tb, (jb + 1) * tb)
            s = _dot_nt(mq_s[rs, sl], mk_ref[jb, :, sl].astype(BF16)) * scale
            p = jnp.exp(s - jnp.max(s, axis=-1, keepdims=True))
            l = jnp.sum(p, axis=-1, keepdims=True)
            o = _dot(p.astype(BF16), mv_ref[jb, :, sl].astype(BF16))
            att_s[rs, sl] = (o / l).astype(BF16)
    o_ref[...] = x + _dot(att_s[...], wo_ref[...])


def _memattn(x, gn, wq, qg, mem_k, mem_v, wo, *, batch):
    n, d = x.shape
    bm, nm, mw = mem_k.shape
    seq = n // batch
    if seq >= ROW_TILE:
        tm, nb, tb = ROW_TILE, 1, ROW_TILE
        per = seq // tm
        mem_idx = lambda i: (i // per, 0, 0)
    else:
        tm = MEM_ROW_TILE
        nb, tb = tm // seq, seq
        mem_idx = lambda i: (i, 0, 0)
    row = pl.BlockSpec((tm, d), lambda i: (i, 0))
    full = lambda a: pl.BlockSpec(a.shape, lambda i: (0,) * a.ndim)
    mem = pl.BlockSpec((nb, nm, mw), mem_idx)
    return pl.pallas_call(
        functools.partial(_memattn_kernel, nb=nb, tb=tb, scale=HEAD_DIM ** -0.5),
        grid=(n // tm,),
        in_specs=[row, full(gn), full(wq), full(qg), mem, mem, full(wo)],
        out_specs=row,
        out_shape=jax.ShapeDtypeStruct((n, d), F32),
        scratch_shapes=[pltpu.VMEM((tm, mw), BF16), pltpu.VMEM((tm, mw), BF16)],
        compiler_params=_cparams("parallel"),
        name="memattn",
    )(x, gn, wq, qg, mem_k, mem_v, wo)


def _layer(x, p, attend, conv_buf, h0, mem_k, mem_v, *, time_major):
    b, t, d = x.shape
    n = b * t
    x0 = x.reshape(n, d)
    x1 = _ffn(x0, p["ffn1_norm"], p["ffn1_wg"], p["ffn1_wu"], p["ffn1_wd"])
    w = h0.shape[-1]
    q, k, v, xr, gr = _inproj(x1, p["mix_norm"], p["w_in"], p["q_norm"], p["k_norm"],
                              w=w, batch=b, time_major=time_major)
    if time_major:
        xr, gr = xr.reshape(n, w), gr.reshape(n, w)
    else:
        to_tm = lambda a: a.reshape(b, t, w).transpose(1, 0, 2).reshape(n, w)
        xr, gr = to_tm(xr), to_tm(gr)
    a_out = attend(q, k, v)
    buf_tm = conv_buf.transpose(1, 0, 2).reshape((CONV_W - 1) * b, w)
    r_out, h_last, tail = _lru(xr, gr, buf_tm, h0, p["conv_w"], p["conv_b"], p["w_gate"],
                               p["lru_ba"], p["lru_bx"], p["lru_lambda"], nb=b)
    new_buf = tail.reshape(CONV_W - 1, b, w).transpose(1, 0, 2)
    if time_major:
        r_out = r_out.reshape(t, b * w)
    else:
        r_out = r_out.reshape(t, b, w).transpose(1, 0, 2).reshape(n, w)
    x2 = _merge(x1, p["mix_norm"], a_out, r_out, p["w_in"], p["w_proj_a"], p["w_proj_b"],
                p["w_out"], batch=b, r_time_major=time_major)
    x3 = _memattn(x2, p["mem_norm_x"], p["w_mq"], p["mq_norm"], mem_k, mem_v, p["w_mo"], batch=b)
    y = _ffn(x3, p["ffn2_norm"], p["ffn2_wg"], p["ffn2_wu"], p["ffn2_wd"], p["final_norm"])
    return y.reshape(b, t, d), k, v, new_buf, h_last


def kernel(x_prompt, x_sample, cache_attn_k, cache_attn_v, state_conv, state_lru, cache_mem_k, cache_mem_v, mem_prompt, ffn1_norm, ffn1_wg, ffn1_wu, ffn1_wd, mix_norm, w_in, q_norm, k_norm, rel_bias, conv_w, conv_b, lru_wa, lru_ba, lru_wx, lru_bx, lru_lambda, w_proj_a, w_proj_b, w_out, mem_norm_x, mem_norm_m, w_mq, w_mk, w_mv, mq_norm, mk_norm, w_mo, ffn2_norm, ffn2_wg, ffn2_wu, ffn2_wd, final_norm):
    depth = ffn1_norm.shape[0]
    bp, sp, d = x_prompt.shape
    bd, td, _ = x_sample.shape
    heads = rel_bias.shape[1]
    w = heads * HEAD_DIM
    mheads = w_mq.shape[-1] // HEAD_DIM
    n_mem = mem_prompt.shape[1]
    n_keep = min(BAND_ROWS, sp)
    yp, ys = x_prompt, x_sample
    outs = [[] for _ in range(10)]
    for l in range(depth):
        bf = lambda a: a[l].astype(BF16)
        row = lambda a: a[l].reshape(1, -1)
        p = {
            "ffn1_norm": row(ffn1_norm), "ffn1_wg": bf(ffn1_wg), "ffn1_wu": bf(ffn1_wu), "ffn1_wd": bf(ffn1_wd),
            "mix_norm": row(mix_norm), "w_in": bf(w_in), "q_norm": row(q_norm), "k_norm": row(k_norm),
            "conv_w": conv_w[l], "conv_b": row(conv_b),
            "w_gate": _pair_gate_weights(lru_wa[l], lru_wx[l]),
            "lru_ba": row(lru_ba), "lru_bx": row(lru_bx), "lru_lambda": row(lru_lambda),
            "w_proj_a": bf(w_proj_a), "w_proj_b": bf(w_proj_b), "w_out": bf(w_out),
            "mem_norm_x": row(mem_norm_x), "w_mq": bf(w_mq), "mq_norm": row(mq_norm), "w_mo": bf(w_mo),
            "ffn2_norm": row(ffn2_norm), "ffn2_wg": bf(ffn2_wg), "ffn2_wu": bf(ffn2_wu), "ffn2_wd": bf(ffn2_wd),
            "final_norm": row(final_norm),
        }
        rb = rel_bias[l]
        mk, mv = _memkv(mem_prompt.reshape(bp * n_mem, d), row(mem_norm_m), bf(w_mk), bf(w_mv), row(mk_norm))
        mk, mv = mk.reshape(bp, n_mem, -1), mv.reshape(bp, n_mem, -1)
        attend_p = lambda q, k, v: _band_attn(q, k, v, rb, batch=bp)
        yp, k_p, v_p, cb_p, hl_p = _layer(
            yp, p, attend_p, jnp.zeros((bp, CONV_W - 1, w), F32), jnp.zeros((bp, w), F32), mk, mv,
            time_major=True)
        keep = lambda a: a.reshape(bp, sp, heads, HEAD_DIM)[:, sp - n_keep:]
        kc = cache_attn_k[l].reshape(bd, -1, w)
        vc = cache_attn_v[l].reshape(bd, -1, w)
        attend_s = lambda q, k, v: _step_attn(q, k, v, rb, kc, vc)
        ys, k_s, v_s, cb_s, hl_s = _layer(
            ys, p, attend_s, state_conv[l], state_lru[l],
            cache_mem_k[l].reshape(bd, n_mem, -1), cache_mem_v[l].reshape(bd, n_mem, -1),
            time_major=False)
        new = [keep(k_p), keep(v_p), cb_p, hl_p,
               mk.reshape(bp, n_mem, mheads, HEAD_DIM), mv.reshape(bp, n_mem, mheads, HEAD_DIM),
               k_s.reshape(bd, td, heads, HEAD_DIM), v_s.reshape(bd, td, heads, HEAD_DIM), cb_s, hl_s]
        for o, a in zip(outs, new):
            o.append(a)
    return (yp, ys) + tuple(jnp.stack(o) for o in outs)
```

```python
import functools

import numpy as np
import jax
import jax.numpy as jnp
from jax import lax
from jax.experimental import pallas as pl
from jax.experimental.pallas import tpu as pltpu

F32 = jnp.float32
BF16 = jnp.bfloat16

CHUNK = 64
LEFT_CHUNKS = 8
BAND_ROWS = LEFT_CHUNKS * CHUNK
HEAD_DIM = 128
PAST_LEN = 1024
CONV_W = 4
LRU_C = 8.0
EPS = 1e-6
NEG_INF = -1e30

VMEM_LIMIT_BYTES = 56 * 1024 * 1024
LANES = 128
SUBLANES = 8

FFN_ROW_TILE = 512
ROW_TILE = 512
FFN_COL_TILE = 512
MERGE_COL_TILE = 512
ATTN_Q_TILE = 256
LRU_TIME_TILE = 64
MEM_ROW_TILE = 128


def _cparams(*sem):
    return pltpu.CompilerParams(dimension_semantics=sem, vmem_limit_bytes=VMEM_LIMIT_BYTES)


def _rms(x, g):
    ms = jnp.mean(x * x, axis=-1, keepdims=True)
    return x * lax.rsqrt(ms + EPS) * g


def _store_head_rms(u, g, o_ref):
    for h in range(u.shape[-1] // HEAD_DIM):
        sl = slice(h * HEAD_DIM, (h + 1) * HEAD_DIM)
        o_ref[:, sl] = _rms(u[:, sl], g).astype(o_ref.dtype)


def _dot(a, b):
    return jnp.dot(a, b, preferred_element_type=F32)


def _dot_nt(a, b):
    return lax.dot_general(a, b, (((1,), (1,)), ((), ())), preferred_element_type=F32)


def _ffn_kernel(*refs, nf, final_norm):
    if final_norm:
        x_ref, gn_ref, wg_ref, wu_ref, wd_ref, gf_ref, o_ref, xn_ref = refs
    else:
        x_ref, gn_ref, wg_ref, wu_ref, wd_ref, o_ref, xn_ref = refs
    f = pl.program_id(1)

    @pl.when(f == 0)
    def _():
        xn_ref[...] = _rms(x_ref[...], gn_ref[...]).astype(BF16)
        o_ref[...] = jnp.zeros_like(o_ref)

    xn = xn_ref[...]
    g = _dot(xn, wg_ref[...])
    u = _dot(xn, wu_ref[...])
    h = (jax.nn.silu(g) * u).astype(BF16)
    o_ref[...] += _dot(h, wd_ref[...])

    @pl.when(f == nf - 1)
    def _():
        y = x_ref[...] + 0.5 * o_ref[...]
        if final_norm:
            y = _rms(y, gf_ref[...])
        o_ref[...] = y


def _ffn(x, gn, wg, wu, wd, final_gain=None):
    n, d = x.shape
    dff = wg.shape[1]
    tm, tf = min(FFN_ROW_TILE, n), FFN_COL_TILE
    nf = dff // tf
    final_norm = final_gain is not None
    row = pl.BlockSpec((tm, d), lambda i, f: (i, 0))
    vec = pl.BlockSpec((1, d), lambda i, f: (0, 0))
    in_specs = [row, vec,
                pl.BlockSpec((d, tf), lambda i, f: (0, f)),
                pl.BlockSpec((d, tf), lambda i, f: (0, f)),
                pl.BlockSpec((tf, d), lambda i, f: (f, 0))]
    args = [x, gn, wg, wu, wd]
    if final_norm:
        in_specs.append(vec)
        args.append(final_gain)
    return pl.pallas_call(
        functools.partial(_ffn_kernel, nf=nf, final_norm=final_norm),
        grid=(n // tm, nf),
        in_specs=in_specs,
        out_specs=row,
        out_shape=jax.ShapeDtypeStruct((n, d), F32),
        scratch_shapes=[pltpu.VMEM((tm, d), BF16)],
        compiler_params=_cparams("parallel", "arbitrary"),
        name="ffn",
    )(*args)


def _inproj_kernel(x_ref, gn_ref, w_ref, qg_ref, kg_ref,
                   q_ref, k_ref, v_ref, xr_ref, gr_ref, xn_ref):
    j = pl.program_id(1)

    @pl.when(j == 0)
    def _():
        xn_ref[...] = _rms(x_ref[...], gn_ref[...]).astype(BF16)

    u = _dot(xn_ref[...], w_ref[...])

    @pl.when(j == 0)
    def _():
        _store_head_rms(u, qg_ref[...], q_ref)

    @pl.when(j == 1)
    def _():
        _store_head_rms(u, kg_ref[...], k_ref)

    @pl.when(j == 2)
    def _():
        v_ref[...] = u

    @pl.when(j == 3)
    def _():
        xr_ref[...] = u

    @pl.when(j == 4)
    def _():
        gr_ref[...] = u


def _inproj(x, gn, w_in, qg, kg, *, w):
    n, d = x.shape
    tm = min(ROW_TILE, n)
    row = pl.BlockSpec((tm, d), lambda i, j: (i, 0))
    out = pl.BlockSpec((tm, w), lambda i, j: (i, 0))
    vec = lambda m: pl.BlockSpec((1, m), lambda i, j: (0, 0))
    f32_out = jax.ShapeDtypeStruct((n, w), F32)
    return pl.pallas_call(
        _inproj_kernel,
        grid=(n // tm, 5),
        in_specs=[row, vec(d), pl.BlockSpec((d, w), lambda i, j: (0, j)),
                  vec(HEAD_DIM), vec(HEAD_DIM)],
        out_specs=[out] * 5,
        out_shape=[jax.ShapeDtypeStruct((n, w), BF16), f32_out, f32_out, f32_out, f32_out],
        scratch_shapes=[pltpu.VMEM((tm, d), BF16)],
        compiler_params=_cparams("parallel", "arbitrary"),
        name="inproj",
    )(x, gn, w_in, qg, kg)


def _memkv_kernel(m_ref, gn_ref, wk_ref, wv_ref, kg_ref, k_ref, v_ref):
    xn = _rms(m_ref[...], gn_ref[...]).astype(BF16)
    _store_head_rms(_dot(xn, wk_ref[...]), kg_ref[...], k_ref)
    v_ref[...] = _dot(xn, wv_ref[...])


def _memkv(m, gn, wk, wv, kg):
    n, d = m.shape
    mw = wk.shape[1]
    tm = min(ROW_TILE, n)
    row = lambda c: pl.BlockSpec((tm, c), lambda i: (i, 0))
    full = lambda a: pl.BlockSpec(a.shape, lambda i: (0, 0))
    return pl.pallas_call(
        _memkv_kernel,
        grid=(n // tm,),
        in_specs=[row(d), full(gn), full(wk), full(wv), full(kg)],
        out_specs=[row(mw), row(mw)],
        out_shape=[jax.ShapeDtypeStruct((n, mw), F32)] * 2,
        compiler_params=_cparams("parallel"),
        name="memkv",
    )(m, gn, wk, wv, kg)


def _band_attn_kernel(q_ref, k_ref, v_ref, g_ref, vis_ref, o_ref, bias_s, kb_s, vb_s,
                      *, tq, win, nt, scale):
    nvar = vis_ref.shape[0]

    @pl.when(pl.program_id(1) == 0)
    def _():
        for var in range(nvar):
            row = jnp.broadcast_to(g_ref[0, var:var + 1, :], (tq, g_ref.shape[-1]))
            toeplitz = pltpu.roll(row, 0, 1, stride=1, stride_axis=0)
            bias_s[var] = jnp.where(vis_ref[var] > 0.5, toeplitz[:, :win], NEG_INF)

    kb_s[...] = k_ref[...].astype(BF16)
    vb_s[...] = v_ref[...].astype(BF16)

    def tile(i, carry):
        q0 = pl.multiple_of(i * tq, tq)
        ws = pl.multiple_of(jnp.maximum(i * tq - BAND_ROWS, 0), tq)
        var = jnp.minimum(i, nvar - 1)
        s = _dot_nt(q_ref[pl.ds(q0, tq), :], kb_s[pl.ds(ws, win), :]) * scale + bias_s[var]
        m = jnp.max(s, axis=-1, keepdims=True)
        p = jnp.exp(s - m)
        l = jnp.sum(p, axis=-1, keepdims=True)
        o = _dot(p.astype(BF16), vb_s[pl.ds(ws, win), :])
        o_ref[pl.ds(q0, tq), :] = (o / l).astype(o_ref.dtype)
        return carry

    lax.fori_loop(0, nt, tile, 0)


def _band_tables(rel_bias, tq):
    max_rel = (rel_bias.shape[-1] - 1) // 2
    win = BAND_ROWS + tq
    span = win + tq
    nvar = BAND_ROWS // tq + 1
    off = np.arange(nvar)[:, None] * tq
    c = np.arange(span)[None, :]
    rel = np.where(c < win, off - c, off + span - c)
    gen = rel_bias.astype(F32)[:, np.clip(rel, -max_rel, max_rel) + max_rel]
    qc = (off[:, :, None] + np.arange(tq)[None, :, None]) // CHUNK
    kc = np.arange(win)[None, None, :] // CHUNK
    vis = (kc <= qc) & (kc >= qc - LEFT_CHUNKS)
    return gen, jnp.asarray(vis.astype(np.float32))


def _band_attn(q, k, v, rel_bias, *, batch):
    n, w = q.shape
    seq = n // batch
    heads = w // HEAD_DIM
    tq = ATTN_Q_TILE
    win = BAND_ROWS + tq
    gen, vis = _band_tables(rel_bias, tq)
    blk = pl.BlockSpec((seq, HEAD_DIM), lambda h, b: (b, h))
    return pl.pallas_call(
        functools.partial(_band_attn_kernel, tq=tq, win=win, nt=seq // tq, scale=HEAD_DIM ** -0.5),
        grid=(heads, batch),
        in_specs=[blk, blk, blk,
                  pl.BlockSpec((1,) + gen.shape[1:], lambda h, b: (h, 0, 0)),
                  pl.BlockSpec(vis.shape, lambda h, b: (0, 0, 0))],
        out_specs=blk,
        out_shape=jax.ShapeDtypeStruct((n, w), BF16),
        scratch_shapes=[pltpu.VMEM(vis.shape, F32), pltpu.VMEM((seq, HEAD_DIM), BF16),
                        pltpu.VMEM((seq, HEAD_DIM), BF16)],
        compiler_params=_cparams("arbitrary", "arbitrary"),
        name="band_attn",
    )(q, k, v, gen, vis)


def _step_attn_kernel(q_ref, kn_ref, vn_ref, kc_ref, vc_ref, bc_ref, bn_ref, o_ref, *, scale):
    q = q_ref[0]
    sc = _dot_nt(q, kc_ref[0].astype(BF16)) * scale + bc_ref[...]
    sn = _dot_nt(q, kn_ref[0].astype(BF16)) * scale + bn_ref[...]
    m = jnp.maximum(jnp.max(sc, axis=-1, keepdims=True), jnp.max(sn, axis=-1, keepdims=True))
    pc = jnp.exp(sc - m)
    pn = jnp.exp(sn - m)
    l = jnp.sum(pc, axis=-1, keepdims=True) + jnp.sum(pn, axis=-1, keepdims=True)
    o = _dot(pc.astype(BF16), vc_ref[0].astype(BF16)) + _dot(pn.astype(BF16), vn_ref[0].astype(BF16))
    o_ref[0] = (o / l).astype(o_ref.dtype)


def _step_attn(q, k, v, rel_bias, k_cache, v_cache):
    bd, wc, heads, hd = k_cache.shape
    t = q.shape[0] // bd
    max_rel = (rel_bias.shape[-1] - 1) // 2
    qpos = PAST_LEN + np.arange(t)
    kpos = np.concatenate([PAST_LEN - wc + np.arange(wc), PAST_LEN + np.arange(t)])
    idx = np.clip(qpos[:, None] - kpos[None, :], -max_rel, max_rel) + max_rel
    qc, kc = qpos[:, None] // CHUNK, kpos[None, :] // CHUNK
    vis = (kc <= qc) & (kc >= qc - LEFT_CHUNKS) & (kpos[None, :] >= 0)
    bias = jnp.where(vis[None], rel_bias.astype(F32)[:, idx], NEG_INF)
    same_head = np.eye(heads, dtype=bool)[None, :, None, :]
    bias = jnp.where(same_head, bias.transpose(1, 0, 2)[:, :, :, None], NEG_INF)
    bias_c = bias[:, :, :wc].reshape(t * heads, wc * heads)
    bias_n = bias[:, :, wc:].reshape(t * heads, t * heads)
    pairs = lambda a: a.reshape(bd, t * heads, hd)
    new = pl.BlockSpec((1, t * heads, hd), lambda b: (b, 0, 0))
    cache = pl.BlockSpec((1, wc * heads, hd), lambda b: (b, 0, 0))
    full = lambda a: pl.BlockSpec(a.shape, lambda b: (0,) * a.ndim)
    out = pl.pallas_call(
        functools.partial(_step_attn_kernel, scale=HEAD_DIM ** -0.5),
        grid=(bd,),
        in_specs=[new, new, new, cache, cache, full(bias_c), full(bias_n)],
        out_specs=new,
        out_shape=jax.ShapeDtypeStruct((bd, t * heads, hd), BF16),
        compiler_params=_cparams("parallel"),
        name="step_attn",
    )(pairs(q), pairs(k), pairs(v), k_cache.reshape(bd, wc * heads, hd),
      v_cache.reshape(bd, wc * heads, hd), bias_c, bias_n)
    return out.reshape(q.shape)


def _lru_kernel(xr_ref, gr_ref, buf0_ref, h0_ref, cw_ref, cb_ref, wgate_ref, ba_ref, bx_ref,
                lam_ref, ro_ref, hl_ref, tail_ref, xp_s, xc_s, a_s, b_s, h_s, *, tt, nb, pitch):
    nslab = a_s.shape[0]
    lane = lambda p: slice(p * LANES, (p + 1) * LANES)
    hist0 = SUBLANES - (CONV_W - 1)

    @pl.when(pl.program_id(0) == 0)
    def _():
        xp_s[:, 0:SUBLANES, :] = buf0_ref[...]
        xc_s[...] = jnp.zeros_like(xc_s)
        for p in range(nslab):
            h_s[p] = h0_ref[:, lane(p)]

    def conv(b, carry):
        xp_s[b, SUBLANES:SUBLANES + tt, :] = xr_ref[b]
        xc = cb_ref[...]
        for j in range(CONV_W):
            xc = xc + xp_s[b, hist0 + j:hist0 + j + tt, :] * cw_ref[j:j + 1, :]
        xc_s[pl.ds(pl.multiple_of(b * pitch, SUBLANES), tt), :] = xc
        xp_s[b, 0:SUBLANES, :] = xp_s[b, tt:tt + SUBLANES, :]
        return carry

    lax.fori_loop(0, nb, conv, 0)
    tail_ref[...] = xp_s[:, 0:SUBLANES, :]

    sp = jax.nn.softplus(-lam_ref[...])
    for p in range(nslab):
        xc = xc_s[:, lane(p)]
        z = _dot(xc.astype(BF16), wgate_ref[p])
        r = jax.nn.sigmoid(z[:, :LANES] + ba_ref[:, lane(p)])
        ig = jax.nn.sigmoid(z[:, LANES:] + bx_ref[:, lane(p)])
        log_a = -LRU_C * r * sp[:, lane(p)]
        a = jnp.exp(log_a)
        a_s[p] = a
        b_s[p] = jnp.sqrt(-jnp.tanh(log_a) * (a * a + 1.0)) * (ig * xc)

    def step(t, hs):
        rows = pl.ds(t, nb, stride=pitch)
        new = []
        for p in range(nslab):
            h = a_s[p, rows, :] * hs[p] + b_s[p, rows, :]
            b_s[p, rows, :] = h
            new.append(h)
        return tuple(new)

    hs = lax.fori_loop(0, tt, step, tuple(h_s[p] for p in range(nslab)))
    for p in range(nslab):
        h_s[p] = hs[p]
        hl_ref[:, lane(p)] = hs[p]

    def gate_out(b, carry):
        r0 = pl.multiple_of(b * pitch, SUBLANES)
        for p in range(nslab):
            ro_ref[b, :, lane(p)] = (jax.nn.gelu(gr_ref[b, :, lane(p)])
                                     * b_s[p, pl.ds(r0, tt), :]).astype(ro_ref.dtype)
        return carry

    lax.fori_loop(0, nb, gate_out, 0)


def _lru(xr, gr, conv_buf, h0, conv_w, conv_b, wgate, ba, bx, lam):
    nb, t, r = xr.shape
    tt = min(LRU_TIME_TILE, t)
    pitch = tt + SUBLANES
    nslab = r // LANES
    buf0 = jnp.pad(conv_buf, ((0, 0), (SUBLANES - (CONV_W - 1), 0), (0, 0)))
    blk = pl.BlockSpec((nb, tt, r), lambda i: (0, i, 0))
    full = lambda a: pl.BlockSpec(a.shape, lambda i: (0,) * a.ndim)
    r_out, h_last, tail = pl.pallas_call(
        functools.partial(_lru_kernel, tt=tt, nb=nb, pitch=pitch),
        grid=(t // tt,),
        in_specs=[blk, blk, full(buf0), full(h0), full(conv_w), full(conv_b), full(wgate),
                  full(ba), full(bx), full(lam)],
        out_specs=[blk, full(h0), full(buf0)],
        out_shape=[jax.ShapeDtypeStruct((nb, t, r), BF16),
                   jax.ShapeDtypeStruct(h0.shape, F32),
                   jax.ShapeDtypeStruct(buf0.shape, F32)],
        scratch_shapes=[pltpu.VMEM((nb, tt + SUBLANES, r), F32),
                        pltpu.VMEM((nb * pitch, r), F32),
                        pltpu.VMEM((nslab, nb * pitch, LANES), F32),
                        pltpu.VMEM((nslab, nb * pitch, LANES), F32),
                        pltpu.VMEM((nslab, nb, LANES), F32)],
        compiler_params=_cparams("arbitrary"),
        name="conv_lru",
    )(xr, gr, buf0, h0, conv_w, conv_b, wgate, ba, bx, lam)
    return r_out, h_last, tail[:, SUBLANES - (CONV_W - 1):]


def _pair_gate_weights(wa, wx):
    nblk, bs, _ = wa.shape
    def pair(w):
        w = w.reshape(nblk // 2, 2, bs, bs)
        z = jnp.zeros_like(w[:, 0])
        top = jnp.concatenate([w[:, 0], z], axis=-1)
        bot = jnp.concatenate([z, w[:, 1]], axis=-1)
        return jnp.concatenate([top, bot], axis=-2)
    return jnp.concatenate([pair(wa), pair(wx)], axis=-1).astype(BF16)


def _merge_kernel(x_ref, gn_ref, a_ref, r_ref, wga_ref, wgb_ref, wpa_ref, wpb_ref, wo_ref,
                  o_ref, xn_ref, *, nc):
    c = pl.program_id(1)

    @pl.when(c == 0)
    def _():
        xn_ref[...] = _rms(x_ref[...], gn_ref[...]).astype(BF16)
        o_ref[...] = x_ref[...]

    xn = xn_ref[...]
    m = (jax.nn.sigmoid(_dot(xn, wga_ref[...])) * _dot(a_ref[...], wpa_ref[...])
         + jax.nn.sigmoid(_dot(xn, wgb_ref[...])) * _dot(r_ref[...], wpb_ref[...]))
    o_ref[...] += _dot(m.astype(BF16), wo_ref[...])


def _merge(x, gn, a_out, r_out, w_in, w_pa, w_pb, w_out):
    n, d = x.shape
    w = a_out.shape[1]
    tm, tc = min(ROW_TILE, n), MERGE_COL_TILE
    nc = d // tc
    ga0 = (w_in.shape[1] - 2 * d) // tc
    gb0 = ga0 + nc
    row = pl.BlockSpec((tm, d), lambda i, c: (i, 0))
    a_spec = pl.BlockSpec((tm, w), lambda i, c: (i, 0))
    return pl.pallas_call(
        functools.partial(_merge_kernel, nc=nc),
        grid=(n // tm, nc),
        in_specs=[row, pl.BlockSpec((1, d), lambda i, c: (0, 0)), a_spec, a_spec,
                  pl.BlockSpec((d, tc), lambda i, c: (0, ga0 + c)),
                  pl.BlockSpec((d, tc), lambda i, c: (0, gb0 + c)),
                  pl.BlockSpec((w, tc), lambda i, c: (0, c)),
                  pl.BlockSpec((w, tc), lambda i, c: (0, c)),
                  pl.BlockSpec((tc, d), lambda i, c: (c, 0))],
        out_specs=row,
        out_shape=jax.ShapeDtypeStruct((n, d), F32),
        scratch_shapes=[pltpu.VMEM((tm, d), BF16)],
        compiler_params=_cparams("parallel", "arbitrary"),
        name="merge",
    )(x, gn, a_out, r_out, w_in, w_in, w_pa, w_pb, w_out)


def _memattn_kernel(x_ref, gn_ref, wq_ref, qg_ref, mk_ref, mv_ref, wo_ref, o_ref, mq_s, att_s,
                    *, nb, tb, scale):
    x = x_ref[...]
    _store_head_rms(_dot(_rms(x, gn_ref[...]).astype(BF16), wq_ref[...]), qg_ref[...], mq_s)
    for h in range(mq_s.shape[-1] // HEAD_DIM):
        sl = slice(h * HEAD_DIM, (h + 1) * HEAD_DIM)
        for jb in range(nb):
            rs = slice(jb * tb, (jb + 1) * tb)
            s = _dot_nt(mq_s[rs, sl], mk_ref[jb, :, sl].astype(BF16)) * scale
            p = jnp.exp(s - jnp.max(s, axis=-1, keepdims=True))
            l = jnp.sum(p, axis=-1, keepdims=True)
            o = _dot(p.astype(BF16), mv_ref[jb, :, sl].astype(BF16))
            att_s[rs, sl] = (o / l).astype(BF16)
    o_ref[...] = x + _dot(att_s[...], wo_ref[...])


def _memattn(x, gn, wq, qg, mem_k, mem_v, wo, *, batch):
    n, d = x.shape
    bm, nm, mw = mem_k.shape
    seq = n // batch
    if seq >= ROW_TILE:
        tm, nb, tb = ROW_TILE, 1, ROW_TILE
        per = seq // tm
        mem_idx = lambda i: (i // per, 0, 0)
    else:
        tm = MEM_ROW_TILE
        nb, tb = tm // seq, seq
        mem_idx = lambda i: (i, 0, 0)
    row = pl.BlockSpec((tm, d), lambda i: (i, 0))
    full = lambda a: pl.BlockSpec(a.shape, lambda i: (0,) * a.ndim)
    mem = pl.BlockSpec((nb, nm, mw), mem_idx)
    return pl.pallas_call(
        functools.partial(_memattn_kernel, nb=nb, tb=tb, scale=HEAD_DIM ** -0.5),
        grid=(n // tm,),
        in_specs=[row, full(gn), full(wq), full(qg), mem, mem, full(wo)],
        out_specs=row,
        out_shape=jax.ShapeDtypeStruct((n, d), F32),
        scratch_shapes=[pltpu.VMEM((tm, mw), BF16), pltpu.VMEM((tm, mw), BF16)],
        compiler_params=_cparams("parallel"),
        name="memattn",
    )(x, gn, wq, qg, mem_k, mem_v, wo)


def _layer(x, p, attend, conv_buf, h0, mem_k, mem_v):
    b, t, d = x.shape
    n = b * t
    w = h0.shape[-1]
    x1 = _ffn(x.reshape(n, d), p["ffn1_norm"], p["ffn1_wg"], p["ffn1_wu"], p["ffn1_wd"])
    q, k, v, xr, gr = _inproj(x1, p["mix_norm"], p["w_in"], p["q_norm"], p["k_norm"], w=w)
    a_out = attend(q, k, v)
    r_out, h_last, new_buf = _lru(xr.reshape(b, t, w), gr.reshape(b, t, w), conv_buf, h0,
                                  p["conv_w"], p["conv_b"], p["w_gate"],
                                  p["lru_ba"], p["lru_bx"], p["lru_lambda"])
    x2 = _merge(x1, p["mix_norm"], a_out, r_out.reshape(n, w), p["w_in"], p["w_proj_a"],
                p["w_proj_b"], p["w_out"])
    x3 = _memattn(x2, p["mem_norm_x"], p["w_mq"], p["mq_norm"], mem_k, mem_v, p["w_mo"], batch=b)
    y = _ffn(x3, p["ffn2_norm"], p["ffn2_wg"], p["ffn2_wu"], p["ffn2_wd"], p["final_norm"])
    return y.reshape(b, t, d), k, v, new_buf, h_last


def kernel(x_prompt, x_sample, cache_attn_k, cache_attn_v, state_conv, state_lru, cache_mem_k, cache_mem_v, mem_prompt, ffn1_norm, ffn1_wg, ffn1_wu, ffn1_wd, mix_norm, w_in, q_norm, k_norm, rel_bias, conv_w, conv_b, lru_wa, lru_ba, lru_wx, lru_bx, lru_lambda, w_proj_a, w_proj_b, w_out, mem_norm_x, mem_norm_m, w_mq, w_mk, w_mv, mq_norm, mk_norm, w_mo, ffn2_norm, ffn2_wg, ffn2_wu, ffn2_wd, final_norm):
    depth = ffn1_norm.shape[0]
    bp, sp, d = x_prompt.shape
    bd, td, _ = x_sample.shape
    heads = rel_bias.shape[1]
    w = heads * HEAD_DIM
    mheads = w_mq.shape[-1] // HEAD_DIM
    n_mem = mem_prompt.shape[1]
    n_keep = min(BAND_ROWS, sp)
    yp, ys = x_prompt, x_sample
    outs = [[] for _ in range(10)]
    for l in range(depth):
        bf = lambda a: a[l].astype(BF16)
        row = lambda a: a[l].reshape(1, -1)
        p = {
            "ffn1_norm": row(ffn1_norm), "ffn1_wg": bf(ffn1_wg), "ffn1_wu": bf(ffn1_wu), "ffn1_wd": bf(ffn1_wd),
            "mix_norm": row(mix_norm), "w_in": bf(w_in), "q_norm": row(q_norm), "k_norm": row(k_norm),
            "conv_w": conv_w[l], "conv_b": row(conv_b),
            "w_gate": _pair_gate_weights(lru_wa[l], lru_wx[l]),
            "lru_ba": row(lru_ba), "lru_bx": row(lru_bx), "lru_lambda": row(lru_lambda),
            "w_proj_a": bf(w_proj_a), "w_proj_b": bf(w_proj_b), "w_out": bf(w_out),
            "mem_norm_x": row(mem_norm_x), "w_mq": bf(w_mq), "mq_norm": row(mq_norm), "w_mo": bf(w_mo),
            "ffn2_norm": row(ffn2_norm), "ffn2_wg": bf(ffn2_wg), "ffn2_wu": bf(ffn2_wu), "ffn2_wd": bf(ffn2_wd),
            "final_norm": row(final_norm),
        }
        rb = rel_bias[l]
        mk, mv = _memkv(mem_prompt.reshape(bp * n_mem, d), row(mem_norm_m), bf(w_mk), bf(w_mv), row(mk_norm))
        mk, mv = mk.reshape(bp, n_mem, -1), mv.reshape(bp, n_mem, -1)
        attend_p = lambda q, k, v: _band_attn(q, k, v, rb, batch=bp)
        yp, k_p, v_p, cb_p, hl_p = _layer(
            yp, p, attend_p, jnp.zeros((bp, CONV_W - 1, w), F32), jnp.zeros((bp, w), F32), mk, mv)
        keep = lambda a: a.reshape(bp, sp, w)[:, sp - n_keep:].reshape(bp, n_keep, heads, HEAD_DIM)
        attend_s = lambda q, k, v: _step_attn(q, k, v, rb, cache_attn_k[l], cache_attn_v[l])
        ys, k_s, v_s, cb_s, hl_s = _layer(
            ys, p, attend_s, state_conv[l], state_lru[l],
            cache_mem_k[l].reshape(bd, n_mem, -1), cache_mem_v[l].reshape(bd, n_mem, -1))
        new = [keep(k_p), keep(v_p), cb_p, hl_p,
               mk.reshape(bp, n_mem, mheads, HEAD_DIM), mv.reshape(bp, n_mem, mheads, HEAD_DIM),
               k_s.reshape(bd, td, heads, HEAD_DIM), v_s.reshape(bd, td, heads, HEAD_DIM), cb_s, hl_s]
        for o, a in zip(outs, new):
            o.append(a)
    return (yp, ys) + tuple(jnp.stack(o) for o in outs)
```

```python
import functools

import numpy as np
import jax
import jax.numpy as jnp
from jax import lax
from jax.experimental import pallas as pl
from jax.experimental.pallas import tpu as pltpu

F32 = jnp.float32
BF16 = jnp.bfloat16

CHUNK = 64
LEFT_CHUNKS = 8
BAND_ROWS = LEFT_CHUNKS * CHUNK
HEAD_DIM = 128
PAST_LEN = 1024
CONV_W = 4
LRU_C = 8.0
EPS = 1e-6
NEG_INF = -1e30

VMEM_LIMIT_BYTES = 56 * 1024 * 1024
LANES = 128
SUBLANES = 8

FFN_ROW_TILE = 512
ROW_TILE = 512
FFN_COL_TILE = 512
MERGE_COL_TILE = 512
ATTN_Q_TILE = 256
LRU_TIME_TILE = 128
MEM_ROW_TILE = 128


def _cparams(*sem):
    return pltpu.CompilerParams(dimension_semantics=sem, vmem_limit_bytes=VMEM_LIMIT_BYTES)


def _rms(x, g):
    ms = jnp.mean(x * x, axis=-1, keepdims=True)
    return x * lax.rsqrt(ms + EPS) * g


def _store_head_rms(u, g, o_ref):
    for h in range(u.shape[-1] // HEAD_DIM):
        sl = slice(h * HEAD_DIM, (h + 1) * HEAD_DIM)
        o_ref[:, sl] = _rms(u[:, sl], g).astype(o_ref.dtype)


def _sigmoid(x):
    return 0.5 * jnp.tanh(0.5 * x) + 0.5


def _dot(a, b):
    return jnp.dot(a, b, preferred_element_type=F32)


def _dot_nt(a, b):
    return lax.dot_general(a, b, (((1,), (1,)), ((), ())), preferred_element_type=F32)


def _ffn_kernel(*refs, nf, final_norm):
    if final_norm:
        x_ref, gn_ref, wg_ref, wu_ref, wd_ref, gf_ref, o_ref, xn_ref = refs
    else:
        x_ref, gn_ref, wg_ref, wu_ref, wd_ref, o_ref, xn_ref = refs
    f = pl.program_id(1)

    @pl.when(f == 0)
    def _():
        x = x_ref[...]
        xn_ref[...] = _rms(x, gn_ref[...]).astype(BF16)
        o_ref[...] = x

    xn = xn_ref[...]
    g = _dot(xn, wg_ref[...])
    u = _dot(xn, wu_ref[...])
    h = (jax.nn.silu(g) * u).astype(BF16)
    o_ref[...] += _dot(h, wd_ref[...])

    if final_norm:
        @pl.when(f == nf - 1)
        def _():
            o_ref[...] = _rms(o_ref[...], gf_ref[...])


def _ffn(x, gn, wg, wu, wd, final_gain=None):
    n, d = x.shape
    dff = wg.shape[1]
    tm, tf = min(FFN_ROW_TILE, n), FFN_COL_TILE
    nf = dff // tf
    final_norm = final_gain is not None
    row = pl.BlockSpec((tm, d), lambda i, f: (i, 0))
    vec = pl.BlockSpec((1, d), lambda i, f: (0, 0))
    in_specs = [row, vec,
                pl.BlockSpec((d, tf), lambda i, f: (0, f)),
                pl.BlockSpec((d, tf), lambda i, f: (0, f)),
                pl.BlockSpec((tf, d), lambda i, f: (f, 0))]
    args = [x, gn, wg, wu, wd]
    if final_norm:
        in_specs.append(vec)
        args.append(final_gain)
    return pl.pallas_call(
        functools.partial(_ffn_kernel, nf=nf, final_norm=final_norm),
        grid=(n // tm, nf),
        in_specs=in_specs,
        out_specs=row,
        out_shape=jax.ShapeDtypeStruct((n, d), F32),
        scratch_shapes=[pltpu.VMEM((tm, d), BF16)],
        compiler_params=_cparams("parallel", "arbitrary"),
        name="ffn",
    )(*args)


def _inproj_kernel(x_ref, gn_ref, w_ref, qg_ref, kg_ref,
                   q_ref, k_ref, v_ref, xr_ref, gr_ref, xn_ref):
    j = pl.program_id(1)

    @pl.when(j == 0)
    def _():
        xn_ref[...] = _rms(x_ref[...], gn_ref[...]).astype(BF16)

    def normed(o_ref, g_ref):
        step = 2 * HEAD_DIM
        for c in range(0, w_ref.shape[1], step):
            u = _dot(xn_ref[...], w_ref[:, c:c + step])
            for h in range(c, c + step, HEAD_DIM):
                o_ref[:, h:h + HEAD_DIM] = _rms(u[:, h - c:h - c + HEAD_DIM], g_ref[...]).astype(o_ref.dtype)

    def plain(o_ref):
        o_ref[...] = _dot(xn_ref[...], w_ref[...])

    pl.when(j == 0)(functools.partial(normed, q_ref, qg_ref))
    pl.when(j == 1)(functools.partial(normed, k_ref, kg_ref))
    pl.when(j == 2)(functools.partial(plain, v_ref))
    pl.when(j == 3)(functools.partial(plain, xr_ref))
    pl.when(j == 4)(functools.partial(plain, gr_ref))


def _inproj(x, gn, w_in, qg, kg, *, w):
    n, d = x.shape
    tm = min(ROW_TILE, n)
    row = pl.BlockSpec((tm, d), lambda i, j: (i, 0))
    out = pl.BlockSpec((tm, w), lambda i, j: (i, 0))
    vec = lambda m: pl.BlockSpec((1, m), lambda i, j: (0, 0))
    f32_out = jax.ShapeDtypeStruct((n, w), F32)
    return pl.pallas_call(
        _inproj_kernel,
        grid=(n // tm, 5),
        in_specs=[row, vec(d), pl.BlockSpec((d, w), lambda i, j: (0, j)),
                  vec(HEAD_DIM), vec(HEAD_DIM)],
        out_specs=[out] * 5,
        out_shape=[jax.ShapeDtypeStruct((n, w), BF16), f32_out, f32_out, f32_out, f32_out],
        scratch_shapes=[pltpu.VMEM((tm, d), BF16)],
        compiler_params=_cparams("parallel", "arbitrary"),
        name="inproj",
    )(x, gn, w_in, qg, kg)


def _memkv_kernel(m_ref, gn_ref, wk_ref, wv_ref, kg_ref, k_ref, v_ref):
    xn = _rms(m_ref[...], gn_ref[...]).astype(BF16)
    _store_head_rms(_dot(xn, wk_ref[...]), kg_ref[...], k_ref)
    v_ref[...] = _dot(xn, wv_ref[...])


def _memkv(m, gn, wk, wv, kg):
    n, d = m.shape
    mw = wk.shape[1]
    tm = min(ROW_TILE, n)
    row = lambda c: pl.BlockSpec((tm, c), lambda i: (i, 0))
    full = lambda a: pl.BlockSpec(a.shape, lambda i: (0, 0))
    return pl.pallas_call(
        _memkv_kernel,
        grid=(n // tm,),
        in_specs=[row(d), full(gn), full(wk), full(wv), full(kg)],
        out_specs=[row(mw), row(mw)],
        out_shape=[jax.ShapeDtypeStruct((n, mw), F32)] * 2,
        compiler_params=_cparams("parallel"),
        name="memkv",
    )(m, gn, wk, wv, kg)


def _band_attn_kernel(q_ref, k_ref, v_ref, g_ref, vis_ref, o_ref, bias_s, kb_s, vb_s,
                      *, tq, win, nt, scale):
    nvar = vis_ref.shape[0]

    @pl.when(pl.program_id(1) == 0)
    def _():
        for var in range(nvar):
            row = jnp.broadcast_to(g_ref[0, var:var + 1, :], (tq, g_ref.shape[-1]))
            toeplitz = pltpu.roll(row, 0, 1, stride=1, stride_axis=0)
            bias_s[var] = jnp.where(vis_ref[var] > 0.5, toeplitz[:, :win], NEG_INF)

    kb_s[...] = k_ref[...].astype(BF16)
    vb_s[...] = v_ref[...].astype(BF16)

    def tile(i, carry):
        q0 = pl.multiple_of(i * tq, tq)
        ws = pl.multiple_of(jnp.maximum(i * tq - BAND_ROWS, 0), tq)
        var = jnp.minimum(i, nvar - 1)
        s = _dot_nt(q_ref[pl.ds(q0, tq), :], kb_s[pl.ds(ws, win), :]) * scale + bias_s[var]
        m = jnp.max(s, axis=-1, keepdims=True)
        p = jnp.exp(s - m)
        l = jnp.sum(p, axis=-1, keepdims=True)
        o = _dot(p.astype(BF16), vb_s[pl.ds(ws, win), :])
        o_ref[pl.ds(q0, tq), :] = (o / l).astype(o_ref.dtype)
        return carry

    lax.fori_loop(0, nt, tile, 0, unroll=2)


def _band_tables(rel_bias, tq):
    max_rel = (rel_bias.shape[-1] - 1) // 2
    win = BAND_ROWS + tq
    span = win + tq
    nvar = BAND_ROWS // tq + 1
    off = np.arange(nvar)[:, None] * tq
    c = np.arange(span)[None, :]
    rel = np.where(c < win, off - c, off + span - c)
    gen = rel_bias.astype(F32)[:, np.clip(rel, -max_rel, max_rel) + max_rel]
    qc = (off[:, :, None] + np.arange(tq)[None, :, None]) // CHUNK
    kc = np.arange(win)[None, None, :] // CHUNK
    vis = (kc <= qc) & (kc >= qc - LEFT_CHUNKS)
    return gen, jnp.asarray(vis.astype(np.float32))


def _band_attn(q, k, v, rel_bias, *, batch):
    n, w = q.shape
    seq = n // batch
    heads = w // HEAD_DIM
    tq = ATTN_Q_TILE
    win = BAND_ROWS + tq
    gen, vis = _band_tables(rel_bias, tq)
    blk = pl.BlockSpec((seq, HEAD_DIM), lambda h, b: (b, h))
    return pl.pallas_call(
        functools.partial(_band_attn_kernel, tq=tq, win=win, nt=seq // tq, scale=HEAD_DIM ** -0.5),
        grid=(heads, batch),
        in_specs=[blk, blk, blk,
                  pl.BlockSpec((1,) + gen.shape[1:], lambda h, b: (h, 0, 0)),
                  pl.BlockSpec(vis.shape, lambda h, b: (0, 0, 0))],
        out_specs=blk,
        out_shape=jax.ShapeDtypeStruct((n, w), BF16),
        scratch_shapes=[pltpu.VMEM(vis.shape, F32), pltpu.VMEM((seq, HEAD_DIM), BF16),
                        pltpu.VMEM((seq, HEAD_DIM), BF16)],
        compiler_params=_cparams("arbitrary", "arbitrary"),
        name="band_attn",
    )(q, k, v, gen, vis)


def _step_attn_kernel(q_ref, kn_ref, vn_ref, kc_ref, vc_ref, bc_ref, bn_ref, o_ref, *, scale):
    q = q_ref[0]
    sc = _dot_nt(q, kc_ref[0].astype(BF16)) * scale + bc_ref[...]
    sn = _dot_nt(q, kn_ref[0].astype(BF16)) * scale + bn_ref[...]
    m = jnp.maximum(jnp.max(sc, axis=-1, keepdims=True), jnp.max(sn, axis=-1, keepdims=True))
    pc = jnp.exp(sc - m)
    pn = jnp.exp(sn - m)
    l = jnp.sum(pc, axis=-1, keepdims=True) + jnp.sum(pn, axis=-1, keepdims=True)
    o = _dot(pc.astype(BF16), vc_ref[0].astype(BF16)) + _dot(pn.astype(BF16), vn_ref[0].astype(BF16))
    o_ref[0] = (o / l).astype(o_ref.dtype)


def _step_attn(q, k, v, rel_bias, k_cache, v_cache):
    bd, wc, heads, hd = k_cache.shape
    t = q.shape[0] // bd
    max_rel = (rel_bias.shape[-1] - 1) // 2
    qpos = PAST_LEN + np.arange(t)
    kpos = np.concatenate([PAST_LEN - wc + np.arange(wc), PAST_LEN + np.arange(t)])
    idx = np.clip(qpos[:, None] - kpos[None, :], -max_rel, max_rel) + max_rel
    qc, kc = qpos[:, None] // CHUNK, kpos[None, :] // CHUNK
    vis = (kc <= qc) & (kc >= qc - LEFT_CHUNKS) & (kpos[None, :] >= 0)
    bias = jnp.where(vis[None], rel_bias.astype(F32)[:, idx], NEG_INF)
    same_head = np.eye(heads, dtype=bool)[None, :, None, :]
    bias = jnp.where(same_head, bias.transpose(1, 0, 2)[:, :, :, None], NEG_INF)
    bias_c = bias[:, :, :wc].reshape(t * heads, wc * heads)
    bias_n = bias[:, :, wc:].reshape(t * heads, t * heads)
    pairs = lambda a: a.reshape(bd, t * heads, hd)
    new = pl.BlockSpec((1, t * heads, hd), lambda b: (b, 0, 0))
    cache = pl.BlockSpec((1, wc * heads, hd), lambda b: (b, 0, 0))
    full = lambda a: pl.BlockSpec(a.shape, lambda b: (0,) * a.ndim)
    out = pl.pallas_call(
        functools.partial(_step_attn_kernel, scale=HEAD_DIM ** -0.5),
        grid=(bd,),
        in_specs=[new, new, new, cache, cache, full(bias_c), full(bias_n)],
        out_specs=new,
        out_shape=jax.ShapeDtypeStruct((bd, t * heads, hd), BF16),
        compiler_params=_cparams("parallel"),
        name="step_attn",
    )(pairs(q), pairs(k), pairs(v), k_cache.reshape(bd, wc * heads, hd),
      v_cache.reshape(bd, wc * heads, hd), bias_c, bias_n)
    return out.reshape(q.shape)


def _lru_kernel(xr_ref, gr_ref, buf0_ref, h0_ref, cw_ref, cb_ref, wgate_ref, ba_ref, bx_ref,
                lam_ref, ro_ref, hl_ref, tail_ref, xp_s, xc_s, a_s, b_s, h_s, *, tt, nb, pitch):
    nslab = a_s.shape[0]
    lane = lambda p: slice(p * LANES, (p + 1) * LANES)
    hist0 = SUBLANES - (CONV_W - 1)

    @pl.when(pl.program_id(0) == 0)
    def _():
        xp_s[:, 0:SUBLANES, :] = buf0_ref[...]
        xc_s[...] = jnp.zeros_like(xc_s)
        for p in range(nslab):
            h_s[p] = h0_ref[:, lane(p)]

    def conv(b, carry):
        xp_s[b, SUBLANES:SUBLANES + tt, :] = xr_ref[b]
        xc = cb_ref[...]
        for j in range(CONV_W):
            xc = xc + xp_s[b, hist0 + j:hist0 + j + tt, :] * cw_ref[j:j + 1, :]
        xc_s[pl.ds(pl.multiple_of(b * pitch, SUBLANES), tt), :] = xc
        xp_s[b, 0:SUBLANES, :] = xp_s[b, tt:tt + SUBLANES, :]
        return carry

    lax.fori_loop(0, nb, conv, 0)
    tail_ref[...] = xp_s[:, 0:SUBLANES, :]

    sp = jax.nn.softplus(-lam_ref[...])
    for p in range(nslab):
        xc = xc_s[:, lane(p)]
        z = _dot(xc.astype(BF16), wgate_ref[p])
        r = _sigmoid(z[:, :LANES] + ba_ref[:, lane(p)])
        ig = _sigmoid(z[:, LANES:] + bx_ref[:, lane(p)])
        log_a = -LRU_C * r * sp[:, lane(p)]
        a = jnp.exp(log_a)
        a_s[p] = a
        m2 = -jnp.tanh(log_a) * (a * a + 1.0)
        mult = jnp.where(m2 > 0.0, m2 * lax.rsqrt(m2), 0.0)
        b_s[p] = mult * (ig * xc)

    def step(t, hs):
        rows = pl.ds(t, nb, stride=pitch)
        new = []
        for p in range(nslab):
            h = a_s[p, rows, :] * hs[p] + b_s[p, rows, :]
            b_s[p, rows, :] = h
            new.append(h)
        return tuple(new)

    hs = lax.fori_loop(0, tt, step, tuple(h_s[p] for p in range(nslab)))
    for p in range(nslab):
        h_s[p] = hs[p]
        hl_ref[:, lane(p)] = hs[p]

    def gate_out(b, carry):
        r0 = pl.multiple_of(b * pitch, SUBLANES)
        for p in range(nslab):
            ro_ref[b, :, lane(p)] = (jax.nn.gelu(gr_ref[b, :, lane(p)])
                                     * b_s[p, pl.ds(r0, tt), :]).astype(ro_ref.dtype)
        return carry

    lax.fori_loop(0, nb, gate_out, 0)


def _lru(xr, gr, conv_buf, h0, conv_w, conv_b, wgate, ba, bx, lam):
    nb, t, r = xr.shape
    tt = min(LRU_TIME_TILE, t)
    pitch = tt + SUBLANES
    nslab = r // LANES
    buf0 = jnp.pad(conv_buf, ((0, 0), (SUBLANES - (CONV_W - 1), 0), (0, 0)))
    blk = pl.BlockSpec((nb, tt, r), lambda i: (0, i, 0))
    full = lambda a: pl.BlockSpec(a.shape, lambda i: (0,) * a.ndim)
    r_out, h_last, tail = pl.pallas_call(
        functools.partial(_lru_kernel, tt=tt, nb=nb, pitch=pitch),
        grid=(t // tt,),
        in_specs=[blk, blk, full(buf0), full(h0), full(conv_w), full(conv_b), full(wgate),
                  full(ba), full(bx), full(lam)],
        out_specs=[blk, full(h0), full(buf0)],
        out_shape=[jax.ShapeDtypeStruct((nb, t, r), BF16),
                   jax.ShapeDtypeStruct(h0.shape, F32),
                   jax.ShapeDtypeStruct(buf0.shape, F32)],
        scratch_shapes=[pltpu.VMEM((nb, tt + SUBLANES, r), F32),
                        pltpu.VMEM((nb * pitch, r), F32),
                        pltpu.VMEM((nslab, nb * pitch, LANES), F32),
                        pltpu.VMEM((nslab, nb * pitch, LANES), F32),
                        pltpu.VMEM((nslab, nb, LANES), F32)],
        compiler_params=_cparams("arbitrary"),
        name="conv_lru",
    )(xr, gr, buf0, h0, conv_w, conv_b, wgate, ba, bx, lam)
    return r_out, h_last, tail[:, SUBLANES - (CONV_W - 1):]


def _pair_gate_weights(wa, wx):
    nblk, bs, _ = wa.shape
    def pair(w):
        w = w.reshape(nblk // 2, 2, bs, bs)
        z = jnp.zeros_like(w[:, 0])
        top = jnp.concatenate([w[:, 0], z], axis=-1)
        bot = jnp.concatenate([z, w[:, 1]], axis=-1)
        return jnp.concatenate([top, bot], axis=-2)
    return jnp.concatenate([pair(wa), pair(wx)], axis=-1).astype(BF16)


def _merge_kernel(x_ref, gn_ref, a_ref, r_ref, wga_ref, wgb_ref, wpa_ref, wpb_ref, wo_ref,
                  o_ref, xn_ref, *, nc):
    c = pl.program_id(1)

    @pl.when(c == 0)
    def _():
        xn_ref[...] = _rms(x_ref[...], gn_ref[...]).astype(BF16)
        o_ref[...] = x_ref[...]

    xn = xn_ref[...]
    m = (jax.nn.sigmoid(_dot(xn, wga_ref[...])) * _dot(a_ref[...], wpa_ref[...])
         + jax.nn.sigmoid(_dot(xn, wgb_ref[...])) * _dot(r_ref[...], wpb_ref[...]))
    o_ref[...] += _dot(m.astype(BF16), wo_ref[...])


def _merge(x, gn, a_out, r_out, w_in, w_pa, w_pb, w_out):
    n, d = x.shape
    w = a_out.shape[1]
    tm, tc = min(ROW_TILE, n), MERGE_COL_TILE
    nc = d // tc
    ga0 = (w_in.shape[1] - 2 * d) // tc
    gb0 = ga0 + nc
    row = pl.BlockSpec((tm, d), lambda i, c: (i, 0))
    a_spec = pl.BlockSpec((tm, w), lambda i, c: (i, 0))
    return pl.pallas_call(
        functools.partial(_merge_kernel, nc=nc),
        grid=(n // tm, nc),
        in_specs=[row, pl.BlockSpec((1, d), lambda i, c: (0, 0)), a_spec, a_spec,
                  pl.BlockSpec((d, tc), lambda i, c: (0, ga0 + c)),
                  pl.BlockSpec((d, tc), lambda i, c: (0, gb0 + c)),
                  pl.BlockSpec((w, tc), lambda i, c: (0, c)),
                  pl.BlockSpec((w, tc), lambda i, c: (0, c)),
                  pl.BlockSpec((tc, d), lambda i, c: (c, 0))],
        out_specs=row,
        out_shape=jax.ShapeDtypeStruct((n, d), F32),
        scratch_shapes=[pltpu.VMEM((tm, d), BF16)],
        compiler_params=_cparams("parallel", "arbitrary"),
        name="merge",
    )(x, gn, a_out, r_out, w_in, w_in, w_pa, w_pb, w_out)


def _memattn_kernel(x_ref, gn_ref, wq_ref, qg_ref, mk_ref, mv_ref, wo_ref, o_ref, mq_s, att_s,
                    *, nb, tb, scale):
    x = x_ref[...]
    _store_head_rms(_dot(_rms(x, gn_ref[...]).astype(BF16), wq_ref[...]), qg_ref[...], mq_s)
    for h in range(mq_s.shape[-1] // HEAD_DIM):
        sl = slice(h * HEAD_DIM, (h + 1) * HEAD_DIM)
        for jb in range(nb):
            rs = slice(jb * tb, (jb + 1) * tb)
            s = _dot_nt(mq_s[rs, sl], mk_ref[jb, :, sl].astype(BF16)) * scale
            p = jnp.exp(s - jnp.max(s, axis=-1, keepdims=True))
            l = jnp.sum(p, axis=-1, keepdims=True)
            o = _dot(p.astype(BF16), mv_ref[jb, :, sl].astype(BF16))
            att_s[rs, sl] = (o / l).astype(BF16)
    o_ref[...] = x + _dot(att_s[...], wo_ref[...])


def _memattn(x, gn, wq, qg, mem_k, mem_v, wo, *, batch):
    n, d = x.shape
    bm, nm, mw = mem_k.shape
    seq = n // batch
    if seq >= ROW_TILE:
        tm, nb, tb = ROW_TILE, 1, ROW_TILE
        per = seq // tm
        mem_idx = lambda i: (i // per, 0, 0)
    else:
        tm = MEM_ROW_TILE
        nb, tb = tm // seq, seq
        mem_idx = lambda i: (i, 0, 0)
    row = pl.BlockSpec((tm, d), lambda i: (i, 0))
    full = lambda a: pl.BlockSpec(a.shape, lambda i: (0,) * a.ndim)
    mem = pl.BlockSpec((nb, nm, mw), mem_idx)
    return pl.pallas_call(
        functools.partial(_memattn_kernel, nb=nb, tb=tb, scale=HEAD_DIM ** -0.5),
        grid=(n // tm,),
        in_specs=[row, full(gn), full(wq), full(qg), mem, mem, full(wo)],
        out_specs=row,
        out_shape=jax.ShapeDtypeStruct((n, d), F32),
        scratch_shapes=[pltpu.VMEM((tm, mw), BF16), pltpu.VMEM((tm, mw), BF16)],
        compiler_params=_cparams("parallel"),
        name="memattn",
    )(x, gn, wq, qg, mem_k, mem_v, wo)


def _layer(x, p, attend, conv_buf, h0, mem_k, mem_v):
    b, t, d = x.shape
    n = b * t
    w = h0.shape[-1]
    x1 = _ffn(x.reshape(n, d), p["ffn1_norm"], p["ffn1_wg"], p["ffn1_wu"], p["ffn1_wd"])
    q, k, v, xr, gr = _inproj(x1, p["mix_norm"], p["w_in"], p["q_norm"], p["k_norm"], w=w)
    a_out = attend(q, k, v)
    r_out, h_last, new_buf = _lru(xr.reshape(b, t, w), gr.reshape(b, t, w), conv_buf, h0,
                                  p["conv_w"], p["conv_b"], p["w_gate"],
                                  p["lru_ba"], p["lru_bx"], p["lru_lambda"])
    x2 = _merge(x1, p["mix_norm"], a_out, r_out.reshape(n, w), p["w_in"], p["w_proj_a"],
                p["w_proj_b"], p["w_out"])
    x3 = _memattn(x2, p["mem_norm_x"], p["w_mq"], p["mq_norm"], mem_k, mem_v, p["w_mo"], batch=b)
    y = _ffn(x3, p["ffn2_norm"], p["ffn2_wg"], p["ffn2_wu"], p["ffn2_wd"], p["final_norm"])
    return y.reshape(b, t, d), k, v, new_buf, h_last


def kernel(x_prompt, x_sample, cache_attn_k, cache_attn_v, state_conv, state_lru, cache_mem_k, cache_mem_v, mem_prompt, ffn1_norm, ffn1_wg, ffn1_wu, ffn1_wd, mix_norm, w_in, q_norm, k_norm, rel_bias, conv_w, conv_b, lru_wa, lru_ba, lru_wx, lru_bx, lru_lambda, w_proj_a, w_proj_b, w_out, mem_norm_x, mem_norm_m, w_mq, w_mk, w_mv, mq_norm, mk_norm, w_mo, ffn2_norm, ffn2_wg, ffn2_wu, ffn2_wd, final_norm):
    depth = ffn1_norm.shape[0]
    bp, sp, d = x_prompt.shape
    bd, td, _ = x_sample.shape
    heads = rel_bias.shape[1]
    w = heads * HEAD_DIM
    mheads = w_mq.shape[-1] // HEAD_DIM
    n_mem = mem_prompt.shape[1]
    n_keep = min(BAND_ROWS, sp)
    yp, ys = x_prompt, x_sample
    outs = [[] for _ in range(10)]
    for l in range(depth):
        bf = lambda a: a[l].astype(BF16)
        half = lambda a: (0.5 * a[l]).astype(BF16)
        row = lambda a: a[l].reshape(1, -1)
        p = {
            "ffn1_norm": row(ffn1_norm), "ffn1_wg": bf(ffn1_wg), "ffn1_wu": bf(ffn1_wu), "ffn1_wd": half(ffn1_wd),
            "mix_norm": row(mix_norm), "w_in": bf(w_in), "q_norm": row(q_norm), "k_norm": row(k_norm),
            "conv_w": conv_w[l], "conv_b": row(conv_b),
            "w_gate": _pair_gate_weights(lru_wa[l], lru_wx[l]),
            "lru_ba": row(lru_ba), "lru_bx": row(lru_bx), "lru_lambda": row(lru_lambda),
            "w_proj_a": bf(w_proj_a), "w_proj_b": bf(w_proj_b), "w_out": bf(w_out),
            "mem_norm_x": row(mem_norm_x), "w_mq": bf(w_mq), "mq_norm": row(mq_norm), "w_mo": bf(w_mo),
            "ffn2_norm": row(ffn2_norm), "ffn2_wg": bf(ffn2_wg), "ffn2_wu": bf(ffn2_wu), "ffn2_wd": half(ffn2_wd),
            "final_norm": row(final_norm),
        }
        rb = rel_bias[l]
        mk, mv = _memkv(mem_prompt.reshape(bp * n_mem, d), row(mem_norm_m), bf(w_mk), bf(w_mv), row(mk_norm))
        mk, mv = mk.reshape(bp, n_mem, -1), mv.reshape(bp, n_mem, -1)
        attend_p = lambda q, k, v: _band_attn(q, k, v, rb, batch=bp)
        yp, k_p, v_p, cb_p, hl_p = _layer(
            yp, p, attend_p, jnp.zeros((bp, CONV_W - 1, w), F32), jnp.zeros((bp, w), F32), mk, mv)
        keep = lambda a: a.reshape(bp, sp, w)[:, sp - n_keep:].reshape(bp, n_keep, heads, HEAD_DIM)
        attend_s = lambda q, k, v: _step_attn(q, k, v, rb, cache_attn_k[l], cache_attn_v[l])
        ys, k_s, v_s, cb_s, hl_s = _layer(
            ys, p, attend_s, state_conv[l], state_lru[l],
            cache_mem_k[l].reshape(bd, n_mem, -1), cache_mem_v[l].reshape(bd, n_mem, -1))
        new = [keep(k_p), keep(v_p), cb_p, hl_p,
               mk.reshape(bp, n_mem, mheads, HEAD_DIM), mv.reshape(bp, n_mem, mheads, HEAD_DIM),
               k_s.reshape(bd, td, heads, HEAD_DIM), v_s.reshape(bd, td, heads, HEAD_DIM), cb_s, hl_s]
        for o, a in zip(outs, new):
            o.append(a)
    return (yp, ys) + tuple(jnp.stack(o) for o in outs)
```

```python
import functools

import numpy as np
import jax
import jax.numpy as jnp
from jax import lax
from jax.experimental import pallas as pl
from jax.experimental.pallas import tpu as pltpu

F32 = jnp.float32
BF16 = jnp.bfloat16

CHUNK = 64
LEFT_CHUNKS = 8
BAND_ROWS = LEFT_CHUNKS * CHUNK
HEAD_DIM = 128
PAST_LEN = 1024
CONV_W = 4
LRU_C = 8.0
EPS = 1e-6
NEG_INF = -1e30

VMEM_LIMIT_BYTES = 56 * 1024 * 1024
LANES = 128
SUBLANES = 8

FFN_ROW_TILE = 1024
INPROJ_ROW_TILE = 1024
ROW_TILE = 512
FFN_COL_TILE = 256
MERGE_COL_TILE = 512
ATTN_Q_TILE = 256
LRU_TIME_TILE = 128
MEM_ROW_TILE = 128


def _cparams(*sem):
    return pltpu.CompilerParams(dimension_semantics=sem, vmem_limit_bytes=VMEM_LIMIT_BYTES)


def _rms(x, g):
    ms = jnp.mean(x * x, axis=-1, keepdims=True)
    return x * lax.rsqrt(ms + EPS) * g


def _store_head_rms(u, g, o_ref):
    for h in range(u.shape[-1] // HEAD_DIM):
        sl = slice(h * HEAD_DIM, (h + 1) * HEAD_DIM)
        o_ref[:, sl] = _rms(u[:, sl], g).astype(o_ref.dtype)


def _sigmoid(x):
    return 0.5 * jnp.tanh(0.5 * x) + 0.5


def _dot(a, b):
    return jnp.dot(a, b, preferred_element_type=F32)


def _dot_nt(a, b):
    return lax.dot_general(a, b, (((1,), (1,)), ((), ())), preferred_element_type=F32)


def _ffn_kernel(*refs, nf, final_norm):
    if final_norm:
        x_ref, gn_ref, wg_ref, wu_ref, wd_ref, gf_ref, o_ref, xn_ref = refs
    else:
        x_ref, gn_ref, wg_ref, wu_ref, wd_ref, o_ref, xn_ref = refs
    f = pl.program_id(1)

    @pl.when(f == 0)
    def _():
        x = x_ref[...]
        xn_ref[...] = _rms(x, gn_ref[...]).astype(BF16)
        o_ref[...] = x

    xn = xn_ref[...]
    g = _dot(xn, wg_ref[...])
    u = _dot(xn, wu_ref[...])
    h = (jax.nn.silu(g) * u).astype(BF16)
    o_ref[...] += _dot(h, wd_ref[...])

    if final_norm:
        @pl.when(f == nf - 1)
        def _():
            o_ref[...] = _rms(o_ref[...], gf_ref[...])


def _ffn(x, gn, wg, wu, wd, final_gain=None):
    n, d = x.shape
    dff = wg.shape[1]
    tm, tf = min(FFN_ROW_TILE, n), FFN_COL_TILE
    nf = dff // tf
    final_norm = final_gain is not None
    row = pl.BlockSpec((tm, d), lambda i, f: (i, 0))
    vec = pl.BlockSpec((1, d), lambda i, f: (0, 0))
    in_specs = [row, vec,
                pl.BlockSpec((d, tf), lambda i, f: (0, f)),
                pl.BlockSpec((d, tf), lambda i, f: (0, f)),
                pl.BlockSpec((tf, d), lambda i, f: (f, 0))]
    args = [x, gn, wg, wu, wd]
    if final_norm:
        in_specs.append(vec)
        args.append(final_gain)
    return pl.pallas_call(
        functools.partial(_ffn_kernel, nf=nf, final_norm=final_norm),
        grid=(n // tm, nf),
        in_specs=in_specs,
        out_specs=row,
        out_shape=jax.ShapeDtypeStruct((n, d), F32),
        scratch_shapes=[pltpu.VMEM((tm, d), BF16)],
        compiler_params=_cparams("parallel", "arbitrary"),
        name="ffn",
    )(*args)


def _inproj_kernel(x_ref, gn_ref, w_ref, qg_ref, kg_ref, q_ref, o_ref, xn_ref):
    j = pl.program_id(1)

    @pl.when(j == 0)
    def _():
        xn_ref[...] = _rms(x_ref[...], gn_ref[...]).astype(BF16)

    def normed(out_ref, g_ref):
        step = 2 * HEAD_DIM
        for c in range(0, w_ref.shape[1], step):
            u = _dot(xn_ref[...], w_ref[:, c:c + step])
            for h in range(c, c + step, HEAD_DIM):
                out_ref[:, h:h + HEAD_DIM] = _rms(u[:, h - c:h - c + HEAD_DIM], g_ref[...]).astype(out_ref.dtype)

    def plain():
        o_ref[...] = _dot(xn_ref[...], w_ref[...])

    pl.when(j == 0)(functools.partial(normed, q_ref, qg_ref))
    pl.when(j == 1)(functools.partial(normed, o_ref, kg_ref))
    pl.when(j >= 2)(plain)


def _inproj(x, gn, w_in, qg, kg, *, w):
    n, d = x.shape
    tm = min(INPROJ_ROW_TILE, n)
    row = pl.BlockSpec((tm, d), lambda i, j: (i, 0))
    vec = lambda m: pl.BlockSpec((1, m), lambda i, j: (0, 0))
    return pl.pallas_call(
        _inproj_kernel,
        grid=(n // tm, 5),
        in_specs=[row, vec(d), pl.BlockSpec((d, w), lambda i, j: (0, j)),
                  vec(HEAD_DIM), vec(HEAD_DIM)],
        out_specs=[pl.BlockSpec((tm, w), lambda i, j: (i, 0)),
                   pl.BlockSpec((tm, w), lambda i, j: (i, jnp.maximum(j - 1, 0)))],
        out_shape=[jax.ShapeDtypeStruct((n, w), BF16), jax.ShapeDtypeStruct((n, 4 * w), F32)],
        scratch_shapes=[pltpu.VMEM((tm, d), BF16)],
        compiler_params=_cparams("parallel", "arbitrary"),
        name="inproj",
    )(x, gn, w_in, qg, kg)


def _memkv_kernel(m_ref, gn_ref, wk_ref, wv_ref, kg_ref, k_ref, v_ref):
    xn = _rms(m_ref[...], gn_ref[...]).astype(BF16)
    _store_head_rms(_dot(xn, wk_ref[...]), kg_ref[...], k_ref)
    v_ref[...] = _dot(xn, wv_ref[...])


def _memkv(m, gn, wk, wv, kg):
    n, d = m.shape
    mw = wk.shape[1]
    tm = min(ROW_TILE, n)
    row = lambda c: pl.BlockSpec((tm, c), lambda i: (i, 0))
    full = lambda a: pl.BlockSpec(a.shape, lambda i: (0, 0))
    return pl.pallas_call(
        _memkv_kernel,
        grid=(n // tm,),
        in_specs=[row(d), full(gn), full(wk), full(wv), full(kg)],
        out_specs=[row(mw), row(mw)],
        out_shape=[jax.ShapeDtypeStruct((n, mw), F32)] * 2,
        compiler_params=_cparams("parallel"),
        name="memkv",
    )(m, gn, wk, wv, kg)


def _band_attn_kernel(q_ref, k_ref, v_ref, g_ref, vis_ref, o_ref, bias_s, kb_s, vb_s,
                      *, tq, win, nt, scale):
    nvar = vis_ref.shape[0]

    @pl.when(pl.program_id(1) == 0)
    def _():
        for var in range(nvar):
            row = jnp.broadcast_to(g_ref[0, var:var + 1, :], (tq, g_ref.shape[-1]))
            toeplitz = pltpu.roll(row, 0, 1, stride=1, stride_axis=0)
            bias_s[var] = jnp.where(vis_ref[var] > 0.5, toeplitz[:, :win], NEG_INF)

    kb_s[...] = k_ref[...].astype(BF16)
    vb_s[...] = v_ref[...].astype(BF16)

    def tile(i, carry):
        q0 = pl.multiple_of(i * tq, tq)
        ws = pl.multiple_of(jnp.maximum(i * tq - BAND_ROWS, 0), tq)
        var = jnp.minimum(i, nvar - 1)
        s = _dot_nt(q_ref[pl.ds(q0, tq), :], kb_s[pl.ds(ws, win), :]) * scale + bias_s[var]
        m = jnp.max(s, axis=-1, keepdims=True)
        p = jnp.exp(s - m)
        l = jnp.sum(p, axis=-1, keepdims=True)
        o = _dot(p.astype(BF16), vb_s[pl.ds(ws, win), :])
        o_ref[pl.ds(q0, tq), :] = (o / l).astype(o_ref.dtype)
        return carry

    lax.fori_loop(0, nt, tile, 0, unroll=4)


def _band_tables(rel_bias, tq):
    max_rel = (rel_bias.shape[-1] - 1) // 2
    win = BAND_ROWS + tq
    span = win + tq
    nvar = BAND_ROWS // tq + 1
    off = np.arange(nvar)[:, None] * tq
    c = np.arange(span)[None, :]
    rel = np.where(c < win, off - c, off + span - c)
    gen = rel_bias.astype(F32)[:, np.clip(rel, -max_rel, max_rel) + max_rel]
    qc = (off[:, :, None] + np.arange(tq)[None, :, None]) // CHUNK
    kc = np.arange(win)[None, None, :] // CHUNK
    vis = (kc <= qc) & (kc >= qc - LEFT_CHUNKS)
    return gen, jnp.asarray(vis.astype(np.float32))


def _band_attn(q, kv, rel_bias, *, batch):
    n, w = q.shape
    seq = n // batch
    heads = w // HEAD_DIM
    tq = ATTN_Q_TILE
    win = BAND_ROWS + tq
    gen, vis = _band_tables(rel_bias, tq)
    blk = pl.BlockSpec((seq, HEAD_DIM), lambda h, b: (b, h))
    v_blk = pl.BlockSpec((seq, HEAD_DIM), lambda h, b: (b, heads + h))
    return pl.pallas_call(
        functools.partial(_band_attn_kernel, tq=tq, win=win, nt=seq // tq, scale=HEAD_DIM ** -0.5),
        grid=(heads, batch),
        in_specs=[blk, blk, v_blk,
                  pl.BlockSpec((1,) + gen.shape[1:], lambda h, b: (h, 0, 0)),
                  pl.BlockSpec(vis.shape, lambda h, b: (0, 0, 0))],
        out_specs=blk,
        out_shape=jax.ShapeDtypeStruct((n, w), BF16),
        scratch_shapes=[pltpu.VMEM(vis.shape, F32), pltpu.VMEM((seq, HEAD_DIM), BF16),
                        pltpu.VMEM((seq, HEAD_DIM), BF16)],
        compiler_params=_cparams("arbitrary", "arbitrary"),
        name="band_attn",
    )(q, kv, kv, gen, vis)


def _step_attn_kernel(q_ref, kn_ref, vn_ref, kc_ref, vc_ref, bc_ref, bn_ref, o_ref, *, scale):
    q = q_ref[0]
    sc = _dot_nt(q, kc_ref[0].astype(BF16)) * scale + bc_ref[...]
    sn = _dot_nt(q, kn_ref[0].astype(BF16)) * scale + bn_ref[...]
    m = jnp.maximum(jnp.max(sc, axis=-1, keepdims=True), jnp.max(sn, axis=-1, keepdims=True))
    pc = jnp.exp(sc - m)
    pn = jnp.exp(sn - m)
    l = jnp.sum(pc, axis=-1, keepdims=True) + jnp.sum(pn, axis=-1, keepdims=True)
    o = _dot(pc.astype(BF16), vc_ref[0].astype(BF16)) + _dot(pn.astype(BF16), vn_ref[0].astype(BF16))
    o_ref[0] = (o / l).astype(o_ref.dtype)


def _step_attn(q, k, v, rel_bias, k_cache, v_cache):
    bd, wc, heads, hd = k_cache.shape
    t = q.shape[0] // bd
    max_rel = (rel_bias.shape[-1] - 1) // 2
    qpos = PAST_LEN + np.arange(t)
    kpos = np.concatenate([PAST_LEN - wc + np.arange(wc), PAST_LEN + np.arange(t)])
    idx = np.clip(qpos[:, None] - kpos[None, :], -max_rel, max_rel) + max_rel
    qc, kc = qpos[:, None] // CHUNK, kpos[None, :] // CHUNK
    vis = (kc <= qc) & (kc >= qc - LEFT_CHUNKS) & (kpos[None, :] >= 0)
    bias = jnp.where(vis[None], rel_bias.astype(F32)[:, idx], NEG_INF)
    same_head = np.eye(heads, dtype=bool)[None, :, None, :]
    bias = jnp.where(same_head, bias.transpose(1, 0, 2)[:, :, :, None], NEG_INF)
    bias_c = bias[:, :, :wc].reshape(t * heads, wc * heads)
    bias_n = bias[:, :, wc:].reshape(t * heads, t * heads)
    pairs = lambda a: a.reshape(bd, t * heads, hd)
    new = pl.BlockSpec((1, t * heads, hd), lambda b: (b, 0, 0))
    cache = pl.BlockSpec((1, wc * heads, hd), lambda b: (b, 0, 0))
    full = lambda a: pl.BlockSpec(a.shape, lambda b: (0,) * a.ndim)
    out = pl.pallas_call(
        functools.partial(_step_attn_kernel, scale=HEAD_DIM ** -0.5),
        grid=(bd,),
        in_specs=[new, new, new, cache, cache, full(bias_c), full(bias_n)],
        out_specs=new,
        out_shape=jax.ShapeDtypeStruct((bd, t * heads, hd), BF16),
        compiler_params=_cparams("parallel"),
        name="step_attn",
    )(pairs(q), pairs(k), pairs(v), k_cache.reshape(bd, wc * heads, hd),
      v_cache.reshape(bd, wc * heads, hd), bias_c, bias_n)
    return out.reshape(q.shape)


def _lru_kernel(xr_ref, gr_ref, buf0_ref, h0_ref, cw_ref, cb_ref, wgate_ref, ba_ref, bx_ref,
                lam_ref, ro_ref, hl_ref, tail_ref, xp_s, xc_s, a_s, b_s, h_s, *, tt, nb, pitch):
    nslab = a_s.shape[0]
    lane = lambda p: slice(p * LANES, (p + 1) * LANES)
    hist0 = SUBLANES - (CONV_W - 1)

    @pl.when(pl.program_id(0) == 0)
    def _():
        xp_s[:, 0:SUBLANES, :] = buf0_ref[...]
        xc_s[...] = jnp.zeros_like(xc_s)
        for p in range(nslab):
            h_s[p] = h0_ref[:, lane(p)]

    def conv(b, carry):
        xp_s[b, SUBLANES:SUBLANES + tt, :] = xr_ref[b]
        xc = cb_ref[...]
        for j in range(CONV_W):
            xc = xc + xp_s[b, hist0 + j:hist0 + j + tt, :] * cw_ref[j:j + 1, :]
        xc_s[pl.ds(pl.multiple_of(b * pitch, SUBLANES), tt), :] = xc
        xp_s[b, 0:SUBLANES, :] = xp_s[b, tt:tt + SUBLANES, :]
        return carry

    lax.fori_loop(0, nb, conv, 0)
    tail_ref[...] = xp_s[:, 0:SUBLANES, :]

    sp = jax.nn.softplus(-lam_ref[...])
    for p in range(nslab):
        xc = xc_s[:, lane(p)]
        z = _dot(xc.astype(BF16), wgate_ref[p])
        r = _sigmoid(z[:, :LANES] + ba_ref[:, lane(p)])
        ig = _sigmoid(z[:, LANES:] + bx_ref[:, lane(p)])
        log_a = -LRU_C * r * sp[:, lane(p)]
        a = jnp.exp(log_a)
        a_s[p] = a
        m2 = -jnp.tanh(log_a) * (a * a + 1.0)
        mult = jnp.where(m2 > 0.0, m2 * lax.rsqrt(m2), 0.0)
        b_s[p] = mult * (ig * xc)

    def step(t, hs):
        rows = pl.ds(t, nb, stride=pitch)
        new = []
        for p in range(nslab):
            h = a_s[p, rows, :] * hs[p] + b_s[p, rows, :]
            b_s[p, rows, :] = h
            new.append(h)
        return tuple(new)

    hs = lax.fori_loop(0, tt, step, tuple(h_s[p] for p in range(nslab)))
    for p in range(nslab):
        h_s[p] = hs[p]
        hl_ref[:, lane(p)] = hs[p]

    def gate_out(b, carry):
        r0 = pl.multiple_of(b * pitch, SUBLANES)
        for p in range(nslab):
            ro_ref[b, :, lane(p)] = (jax.nn.gelu(gr_ref[b, :, lane(p)])
                                     * b_s[p, pl.ds(r0, tt), :]).astype(ro_ref.dtype)
        return carry

    lax.fori_loop(0, nb, gate_out, 0)


def _lru(proj, conv_buf, h0, conv_w, conv_b, wgate, ba, bx, lam):
    nb, t, _ = proj.shape
    r = h0.shape[-1]
    tt = min(LRU_TIME_TILE, t)
    pitch = tt + SUBLANES
    nslab = r // LANES
    buf0 = jnp.pad(conv_buf, ((0, 0), (SUBLANES - (CONV_W - 1), 0), (0, 0)))
    blk = pl.BlockSpec((nb, tt, r), lambda i: (0, i, 0))
    xr_blk = pl.BlockSpec((nb, tt, r), lambda i: (0, i, 2))
    gr_blk = pl.BlockSpec((nb, tt, r), lambda i: (0, i, 3))
    full = lambda a: pl.BlockSpec(a.shape, lambda i: (0,) * a.ndim)
    r_out, h_last, tail = pl.pallas_call(
        functools.partial(_lru_kernel, tt=tt, nb=nb, pitch=pitch),
        grid=(t // tt,),
        in_specs=[xr_blk, gr_blk, full(buf0), full(h0), full(conv_w), full(conv_b), full(wgate),
                  full(ba), full(bx), full(lam)],
        out_specs=[blk, full(h0), full(buf0)],
        out_shape=[jax.ShapeDtypeStruct((nb, t, r), BF16),
                   jax.ShapeDtypeStruct(h0.shape, F32),
                   jax.ShapeDtypeStruct(buf0.shape, F32)],
        scratch_shapes=[pltpu.VMEM((nb, tt + SUBLANES, r), F32),
                        pltpu.VMEM((nb * pitch, r), F32),
                        pltpu.VMEM((nslab, nb * pitch, LANES), F32),
                        pltpu.VMEM((nslab, nb * pitch, LANES), F32),
                        pltpu.VMEM((nslab, nb, LANES), F32)],
        compiler_params=_cparams("arbitrary"),
        name="conv_lru",
    )(proj, proj, buf0, h0, conv_w, conv_b, wgate, ba, bx, lam)
    return r_out, h_last, tail[:, SUBLANES - (CONV_W - 1):]


def _pair_gate_weights(wa, wx):
    nblk, bs, _ = wa.shape
    def pair(w):
        w = w.reshape(nblk // 2, 2, bs, bs)
        z = jnp.zeros_like(w[:, 0])
        top = jnp.concatenate([w[:, 0], z], axis=-1)
        bot = jnp.concatenate([z, w[:, 1]], axis=-1)
        return jnp.concatenate([top, bot], axis=-2)
    return jnp.concatenate([pair(wa), pair(wx)], axis=-1).astype(BF16)


def _merge_kernel(x_ref, gn_ref, a_ref, r_ref, wga_ref, wgb_ref, wpa_ref, wpb_ref, wo_ref,
                  o_ref, xn_ref, *, nc):
    c = pl.program_id(1)

    @pl.when(c == 0)
    def _():
        xn_ref[...] = _rms(x_ref[...], gn_ref[...]).astype(BF16)
        o_ref[...] = x_ref[...]

    xn = xn_ref[...]
    m = (jax.nn.sigmoid(_dot(xn, wga_ref[...])) * _dot(a_ref[...], wpa_ref[...])
         + jax.nn.sigmoid(_dot(xn, wgb_ref[...])) * _dot(r_ref[...], wpb_ref[...]))
    o_ref[...] += _dot(m.astype(BF16), wo_ref[...])


def _merge(x, gn, a_out, r_out, w_in, w_pa, w_pb, w_out):
    n, d = x.shape
    w = a_out.shape[1]
    tm, tc = min(ROW_TILE, n), MERGE_COL_TILE
    nc = d // tc
    ga0 = (w_in.shape[1] - 2 * d) // tc
    gb0 = ga0 + nc
    row = pl.BlockSpec((tm, d), lambda i, c: (i, 0))
    a_spec = pl.BlockSpec((tm, w), lambda i, c: (i, 0))
    return pl.pallas_call(
        functools.partial(_merge_kernel, nc=nc),
        grid=(n // tm, nc),
        in_specs=[row, pl.BlockSpec((1, d), lambda i, c: (0, 0)), a_spec, a_spec,
                  pl.BlockSpec((d, tc), lambda i, c: (0, ga0 + c)),
                  pl.BlockSpec((d, tc), lambda i, c: (0, gb0 + c)),
                  pl.BlockSpec((w, tc), lambda i, c: (0, c)),
                  pl.BlockSpec((w, tc), lambda i, c: (0, c)),
                  pl.BlockSpec((tc, d), lambda i, c: (c, 0))],
        out_specs=row,
        out_shape=jax.ShapeDtypeStruct((n, d), F32),
        scratch_shapes=[pltpu.VMEM((tm, d), BF16)],
        compiler_params=_cparams("parallel", "arbitrary"),
        name="merge",
    )(x, gn, a_out, r_out, w_in, w_in, w_pa, w_pb, w_out)


def _memattn_kernel(x_ref, gn_ref, wq_ref, qg_ref, mk_ref, mv_ref, wo_ref, o_ref, mq_s, att_s,
                    *, nb, tb, scale):
    x = x_ref[...]
    _store_head_rms(_dot(_rms(x, gn_ref[...]).astype(BF16), wq_ref[...]), qg_ref[...], mq_s)
    for h in range(mq_s.shape[-1] // HEAD_DIM):
        sl = slice(h * HEAD_DIM, (h + 1) * HEAD_DIM)
        for jb in range(nb):
            rs = slice(jb * tb, (jb + 1) * tb)
            s = _dot_nt(mq_s[rs, sl], mk_ref[jb, :, sl].astype(BF16)) * scale
            p = jnp.exp(s - jnp.max(s, axis=-1, keepdims=True))
            l = jnp.sum(p, axis=-1, keepdims=True)
            o = _dot(p.astype(BF16), mv_ref[jb, :, sl].astype(BF16))
            att_s[rs, sl] = (o / l).astype(BF16)
    o_ref[...] = x + _dot(att_s[...], wo_ref[...])


def _memattn(x, gn, wq, qg, mem_k, mem_v, wo, *, batch):
    n, d = x.shape
    bm, nm, mw = mem_k.shape
    seq = n // batch
    if seq >= ROW_TILE:
        tm, nb, tb = ROW_TILE, 1, ROW_TILE
        per = seq // tm
        mem_idx = lambda i: (i // per, 0, 0)
    else:
        tm = MEM_ROW_TILE
        nb, tb = tm // seq, seq
        mem_idx = lambda i: (i, 0, 0)
    row = pl.BlockSpec((tm, d), lambda i: (i, 0))
    full = lambda a: pl.BlockSpec(a.shape, lambda i: (0,) * a.ndim)
    mem = pl.BlockSpec((nb, nm, mw), mem_idx)
    return pl.pallas_call(
        functools.partial(_memattn_kernel, nb=nb, tb=tb, scale=HEAD_DIM ** -0.5),
        grid=(n // tm,),
        in_specs=[row, full(gn), full(wq), full(qg), mem, mem, full(wo)],
        out_specs=row,
        out_shape=jax.ShapeDtypeStruct((n, d), F32),
        scratch_shapes=[pltpu.VMEM((tm, mw), BF16), pltpu.VMEM((tm, mw), BF16)],
        compiler_params=_cparams("parallel"),
        name="memattn",
    )(x, gn, wq, qg, mem_k, mem_v, wo)


def _layer(x, p, attend, conv_buf, h0, mem_k, mem_v):
    b, t, d = x.shape
    n = b * t
    w = h0.shape[-1]
    x1 = _ffn(x.reshape(n, d), p["ffn1_norm"], p["ffn1_wg"], p["ffn1_wu"], p["ffn1_wd"])
    q, proj = _inproj(x1, p["mix_norm"], p["w_in"], p["q_norm"], p["k_norm"], w=w)
    a_out = attend(q, proj)
    r_out, h_last, new_buf = _lru(proj.reshape(b, t, 4 * w), conv_buf, h0,
                                  p["conv_w"], p["conv_b"], p["w_gate"],
                                  p["lru_ba"], p["lru_bx"], p["lru_lambda"])
    x2 = _merge(x1, p["mix_norm"], a_out, r_out.reshape(n, w), p["w_in"], p["w_proj_a"],
                p["w_proj_b"], p["w_out"])
    x3 = _memattn(x2, p["mem_norm_x"], p["w_mq"], p["mq_norm"], mem_k, mem_v, p["w_mo"], batch=b)
    y = _ffn(x3, p["ffn2_norm"], p["ffn2_wg"], p["ffn2_wu"], p["ffn2_wd"], p["final_norm"])
    kv = proj.reshape(b, t, 4 * w)
    return y.reshape(b, t, d), kv, new_buf, h_last


def kernel(x_prompt, x_sample, cache_attn_k, cache_attn_v, state_conv, state_lru, cache_mem_k, cache_mem_v, mem_prompt, ffn1_norm, ffn1_wg, ffn1_wu, ffn1_wd, mix_norm, w_in, q_norm, k_norm, rel_bias, conv_w, conv_b, lru_wa, lru_ba, lru_wx, lru_bx, lru_lambda, w_proj_a, w_proj_b, w_out, mem_norm_x, mem_norm_m, w_mq, w_mk, w_mv, mq_norm, mk_norm, w_mo, ffn2_norm, ffn2_wg, ffn2_wu, ffn2_wd, final_norm):
    depth = ffn1_norm.shape[0]
    bp, sp, d = x_prompt.shape
    bd, td, _ = x_sample.shape
    heads = rel_bias.shape[1]
    w = heads * HEAD_DIM
    mheads = w_mq.shape[-1] // HEAD_DIM
    n_mem = mem_prompt.shape[1]
    n_keep = min(BAND_ROWS, sp)
    yp, ys = x_prompt, x_sample
    outs = [[] for _ in range(10)]
    for l in range(depth):
        bf = lambda a: a[l].astype(BF16)
        half = lambda a: (0.5 * a[l]).astype(BF16)
        row = lambda a: a[l].reshape(1, -1)
        p = {
            "ffn1_norm": row(ffn1_norm), "ffn1_wg": bf(ffn1_wg), "ffn1_wu": bf(ffn1_wu), "ffn1_wd": half(ffn1_wd),
            "mix_norm": row(mix_norm), "w_in": bf(w_in), "q_norm": row(q_norm), "k_norm": row(k_norm),
            "conv_w": conv_w[l], "conv_b": row(conv_b),
            "w_gate": _pair_gate_weights(lru_wa[l], lru_wx[l]),
            "lru_ba": row(lru_ba), "lru_bx": row(lru_bx), "lru_lambda": row(lru_lambda),
            "w_proj_a": bf(w_proj_a), "w_proj_b": bf(w_proj_b), "w_out": bf(w_out),
            "mem_norm_x": row(mem_norm_x), "w_mq": bf(w_mq), "mq_norm": row(mq_norm), "w_mo": bf(w_mo),
            "ffn2_norm": row(ffn2_norm), "ffn2_wg": bf(ffn2_wg), "ffn2_wu": bf(ffn2_wu), "ffn2_wd": half(ffn2_wd),
            "final_norm": row(final_norm),
        }
        rb = rel_bias[l]
        mk, mv = _memkv(mem_prompt.reshape(bp * n_mem, d), row(mem_norm_m), bf(w_mk), bf(w_mv), row(mk_norm))
        mk, mv = mk.reshape(bp, n_mem, -1), mv.reshape(bp, n_mem, -1)
        attend_p = lambda q, proj: _band_attn(q, proj, rb, batch=bp)
        yp, kv_p, cb_p, hl_p = _layer(
            yp, p, attend_p, jnp.zeros((bp, CONV_W - 1, w), F32), jnp.zeros((bp, w), F32), mk, mv)
        heads_of = lambda a: a.reshape(a.shape[:2] + (heads, HEAD_DIM))
        kv_p = kv_p[:, sp - n_keep:, :2 * w]
        attend_s = lambda q, proj: _step_attn(q, proj[:, :w], proj[:, w:2 * w], rb,
                                              cache_attn_k[l], cache_attn_v[l])
        ys, kv_s, cb_s, hl_s = _layer(
            ys, p, attend_s, state_conv[l], state_lru[l],
            cache_mem_k[l].reshape(bd, n_mem, -1), cache_mem_v[l].reshape(bd, n_mem, -1))
        new = [heads_of(kv_p[..., :w]), heads_of(kv_p[..., w:]), cb_p, hl_p,
               mk.reshape(bp, n_mem, mheads, HEAD_DIM), mv.reshape(bp, n_mem, mheads, HEAD_DIM),
               heads_of(kv_s[..., :w]), heads_of(kv_s[..., w:2 * w]), cb_s, hl_s]
        for o, a in zip(outs, new):
            o.append(a)
    return (yp, ys) + tuple(jnp.stack(o) for o in outs)
```

```python
import functools

import numpy as np
import jax
import jax.numpy as jnp
from jax import lax
from jax.experimental import pallas as pl
from jax.experimental.pallas import tpu as pltpu

F32 = jnp.float32
BF16 = jnp.bfloat16

CHUNK = 64
LEFT_CHUNKS = 8
BAND_ROWS = LEFT_CHUNKS * CHUNK
HEAD_DIM = 128
PAST_LEN = 1024
CONV_W = 4
LRU_C = 8.0
EPS = 1e-6
NEG_INF = -1e30

VMEM_LIMIT_BYTES = 56 * 1024 * 1024
LANES = 128
SUBLANES = 8

FFN_ROW_TILE = 1024
INPROJ_ROW_TILE = 1024
ROW_TILE = 512
FFN_COL_TILE = 512
MERGE_COL_TILE = 512
ATTN_Q_TILE = 256
LRU_TIME_TILE = 128
MEMATTN_ROW_TILE = 1024
MEM_ROW_TILE = 128


def _cparams(*sem):
    return pltpu.CompilerParams(dimension_semantics=sem, vmem_limit_bytes=VMEM_LIMIT_BYTES)


def _rms(x, g):
    ms = jnp.mean(x * x, axis=-1, keepdims=True)
    return x * lax.rsqrt(ms + EPS) * g


def _store_head_rms(u, g, o_ref):
    for h in range(u.shape[-1] // HEAD_DIM):
        sl = slice(h * HEAD_DIM, (h + 1) * HEAD_DIM)
        o_ref[:, sl] = _rms(u[:, sl], g).astype(o_ref.dtype)


def _sigmoid(x):
    return 0.5 * jnp.tanh(0.5 * x) + 0.5


def _dot(a, b):
    return jnp.dot(a, b, preferred_element_type=F32)


def _dot_nt(a, b):
    return lax.dot_general(a, b, (((1,), (1,)), ((), ())), preferred_element_type=F32)


def _ffn_kernel(*refs, nf, final_norm):
    if final_norm:
        x_ref, gn_ref, wg_ref, wu_ref, wd_ref, gf_ref, o_ref, xn_ref = refs
    else:
        x_ref, gn_ref, wg_ref, wu_ref, wd_ref, o_ref, xn_ref = refs
    f = pl.program_id(1)

    @pl.when(f == 0)
    def _():
        x = x_ref[...]
        xn_ref[...] = _rms(x, gn_ref[...]).astype(BF16)
        o_ref[...] = x

    xn = xn_ref[...]
    g = _dot(xn, wg_ref[...])
    u = _dot(xn, wu_ref[...])
    h = (jax.nn.silu(g) * u).astype(BF16)
    o_ref[...] += _dot(h, wd_ref[...])

    if final_norm:
        @pl.when(f == nf - 1)
        def _():
            o_ref[...] = _rms(o_ref[...], gf_ref[...])


def _ffn(x, gn, wg, wu, wd, final_gain=None):
    n, d = x.shape
    dff = wg.shape[1]
    tm, tf = min(FFN_ROW_TILE, n), FFN_COL_TILE
    nf = dff // tf
    final_norm = final_gain is not None
    row = pl.BlockSpec((tm, d), lambda i, f: (i, 0))
    vec = pl.BlockSpec((1, d), lambda i, f: (0, 0))
    in_specs = [row, vec,
                pl.BlockSpec((d, tf), lambda i, f: (0, f)),
                pl.BlockSpec((d, tf), lambda i, f: (0, f)),
                pl.BlockSpec((tf, d), lambda i, f: (f, 0))]
    args = [x, gn, wg, wu, wd]
    if final_norm:
        in_specs.append(vec)
        args.append(final_gain)
    return pl.pallas_call(
        functools.partial(_ffn_kernel, nf=nf, final_norm=final_norm),
        grid=(n // tm, nf),
        in_specs=in_specs,
        out_specs=row,
        out_shape=jax.ShapeDtypeStruct((n, d), F32),
        scratch_shapes=[pltpu.VMEM((tm, d), BF16)],
        compiler_params=_cparams("parallel", "arbitrary"),
        name="ffn",
    )(*args)


def _inproj_kernel(x_ref, gn_ref, w_ref, qg_ref, kg_ref, q_ref, o_ref, xn_ref):
    j = pl.program_id(1)

    @pl.when(j == 0)
    def _():
        xn_ref[...] = _rms(x_ref[...], gn_ref[...]).astype(BF16)

    def normed(out_ref, g_ref):
        step = 2 * HEAD_DIM
        for c in range(0, w_ref.shape[1], step):
            u = _dot(xn_ref[...], w_ref[:, c:c + step])
            for h in range(c, c + step, HEAD_DIM):
                out_ref[:, h:h + HEAD_DIM] = _rms(u[:, h - c:h - c + HEAD_DIM], g_ref[...]).astype(out_ref.dtype)

    def plain():
        o_ref[...] = _dot(xn_ref[...], w_ref[...])

    pl.when(j == 0)(functools.partial(normed, q_ref, qg_ref))
    pl.when(j == 1)(functools.partial(normed, o_ref, kg_ref))
    pl.when(j >= 2)(plain)


def _inproj(x, gn, w_in, qg, kg, *, w):
    n, d = x.shape
    tm = min(INPROJ_ROW_TILE, n)
    row = pl.BlockSpec((tm, d), lambda i, j: (i, 0))
    vec = lambda m: pl.BlockSpec((1, m), lambda i, j: (0, 0))
    return pl.pallas_call(
        _inproj_kernel,
        grid=(n // tm, 5),
        in_specs=[row, vec(d), pl.BlockSpec((d, w), lambda i, j: (0, j)),
                  vec(HEAD_DIM), vec(HEAD_DIM)],
        out_specs=[pl.BlockSpec((tm, w), lambda i, j: (i, 0)),
                   pl.BlockSpec((tm, w), lambda i, j: (i, jnp.maximum(j - 1, 0)))],
        out_shape=[jax.ShapeDtypeStruct((n, w), BF16), jax.ShapeDtypeStruct((n, 4 * w), F32)],
        scratch_shapes=[pltpu.VMEM((tm, d), BF16)],
        compiler_params=_cparams("parallel", "arbitrary"),
        name="inproj",
    )(x, gn, w_in, qg, kg)


def _memkv_kernel(m_ref, gn_ref, wk_ref, wv_ref, kg_ref, k_ref, v_ref):
    xn = _rms(m_ref[...], gn_ref[...]).astype(BF16)
    _store_head_rms(_dot(xn, wk_ref[...]), kg_ref[...], k_ref)
    v_ref[...] = _dot(xn, wv_ref[...])


def _memkv(m, gn, wk, wv, kg):
    n, d = m.shape
    mw = wk.shape[1]
    tm = min(ROW_TILE, n)
    row = lambda c: pl.BlockSpec((tm, c), lambda i: (i, 0))
    full = lambda a: pl.BlockSpec(a.shape, lambda i: (0, 0))
    return pl.pallas_call(
        _memkv_kernel,
        grid=(n // tm,),
        in_specs=[row(d), full(gn), full(wk), full(wv), full(kg)],
        out_specs=[row(mw), row(mw)],
        out_shape=[jax.ShapeDtypeStruct((n, mw), F32)] * 2,
        compiler_params=_cparams("parallel"),
        name="memkv",
    )(m, gn, wk, wv, kg)


def _band_attn_kernel(q_ref, k_ref, v_ref, g_ref, vis_ref, o_ref, bias_s, kb_s, vb_s,
                      *, tq, win, nt, scale):
    nvar = vis_ref.shape[0]

    @pl.when(pl.program_id(1) == 0)
    def _():
        for var in range(nvar):
            row = jnp.broadcast_to(g_ref[0, var:var + 1, :], (tq, g_ref.shape[-1]))
            toeplitz = pltpu.roll(row, 0, 1, stride=1, stride_axis=0)
            bias_s[var] = jnp.where(vis_ref[var] > 0.5, toeplitz[:, :win], NEG_INF)

    kb_s[...] = k_ref[...].astype(BF16)
    vb_s[...] = v_ref[...].astype(BF16)

    def tile(i, carry):
        q0 = pl.multiple_of(i * tq, tq)
        ws = pl.multiple_of(jnp.maximum(i * tq - BAND_ROWS, 0), tq)
        var = jnp.minimum(i, nvar - 1)
        s = _dot_nt(q_ref[pl.ds(q0, tq), :], kb_s[pl.ds(ws, win), :]) * scale + bias_s[var]
        m = jnp.max(s, axis=-1, keepdims=True)
        p = jnp.exp(s - m)
        l = jnp.sum(p, axis=-1, keepdims=True)
        o = _dot(p.astype(BF16), vb_s[pl.ds(ws, win), :])
        o_ref[pl.ds(q0, tq), :] = (o / l).astype(o_ref.dtype)
        return carry

    lax.fori_loop(0, nt, tile, 0, unroll=4)


def _band_tables(rel_bias, tq):
    max_rel = (rel_bias.shape[-1] - 1) // 2
    win = BAND_ROWS + tq
    span = win + tq
    nvar = BAND_ROWS // tq + 1
    off = np.arange(nvar)[:, None] * tq
    c = np.arange(span)[None, :]
    rel = np.where(c < win, off - c, off + span - c)
    gen = rel_bias.astype(F32)[:, np.clip(rel, -max_rel, max_rel) + max_rel]
    qc = (off[:, :, None] + np.arange(tq)[None, :, None]) // CHUNK
    kc = np.arange(win)[None, None, :] // CHUNK
    vis = (kc <= qc) & (kc >= qc - LEFT_CHUNKS)
    return gen, jnp.asarray(vis.astype(np.float32))


def _band_attn(q, kv, rel_bias, *, batch):
    n, w = q.shape
    seq = n // batch
    heads = w // HEAD_DIM
    tq = ATTN_Q_TILE
    win = BAND_ROWS + tq
    gen, vis = _band_tables(rel_bias, tq)
    blk = pl.BlockSpec((seq, HEAD_DIM), lambda h, b: (b, h))
    v_blk = pl.BlockSpec((seq, HEAD_DIM), lambda h, b: (b, heads + h))
    return pl.pallas_call(
        functools.partial(_band_attn_kernel, tq=tq, win=win, nt=seq // tq, scale=HEAD_DIM ** -0.5),
        grid=(heads, batch),
        in_specs=[blk, blk, v_blk,
                  pl.BlockSpec((1,) + gen.shape[1:], lambda h, b: (h, 0, 0)),
                  pl.BlockSpec(vis.shape, lambda h, b: (0, 0, 0))],
        out_specs=blk,
        out_shape=jax.ShapeDtypeStruct((n, w), BF16),
        scratch_shapes=[pltpu.VMEM(vis.shape, F32), pltpu.VMEM((seq, HEAD_DIM), BF16),
                        pltpu.VMEM((seq, HEAD_DIM), BF16)],
        compiler_params=_cparams("arbitrary", "arbitrary"),
        name="band_attn",
    )(q, kv, kv, gen, vis)


def _step_attn_kernel(q_ref, kn_ref, vn_ref, kc_ref, vc_ref, bc_ref, bn_ref, o_ref, *, scale):
    q = q_ref[0]
    sc = _dot_nt(q, kc_ref[0].astype(BF16)) * scale + bc_ref[...]
    sn = _dot_nt(q, kn_ref[0].astype(BF16)) * scale + bn_ref[...]
    m = jnp.maximum(jnp.max(sc, axis=-1, keepdims=True), jnp.max(sn, axis=-1, keepdims=True))
    pc = jnp.exp(sc - m)
    pn = jnp.exp(sn - m)
    l = jnp.sum(pc, axis=-1, keepdims=True) + jnp.sum(pn, axis=-1, keepdims=True)
    o = _dot(pc.astype(BF16), vc_ref[0].astype(BF16)) + _dot(pn.astype(BF16), vn_ref[0].astype(BF16))
    o_ref[0] = (o / l).astype(o_ref.dtype)


def _step_attn(q, k, v, rel_bias, k_cache, v_cache):
    bd, wc, heads, hd = k_cache.shape
    t = q.shape[0] // bd
    max_rel = (rel_bias.shape[-1] - 1) // 2
    qpos = PAST_LEN + np.arange(t)
    kpos = np.concatenate([PAST_LEN - wc + np.arange(wc), PAST_LEN + np.arange(t)])
    idx = np.clip(qpos[:, None] - kpos[None, :], -max_rel, max_rel) + max_rel
    qc, kc = qpos[:, None] // CHUNK, kpos[None, :] // CHUNK
    vis = (kc <= qc) & (kc >= qc - LEFT_CHUNKS) & (kpos[None, :] >= 0)
    bias = jnp.where(vis[None], rel_bias.astype(F32)[:, idx], NEG_INF)
    same_head = np.eye(heads, dtype=bool)[None, :, None, :]
    bias = jnp.where(same_head, bias.transpose(1, 0, 2)[:, :, :, None], NEG_INF)
    bias_c = bias[:, :, :wc].reshape(t * heads, wc * heads)
    bias_n = bias[:, :, wc:].reshape(t * heads, t * heads)
    pairs = lambda a: a.reshape(bd, t * heads, hd)
    new = pl.BlockSpec((1, t * heads, hd), lambda b: (b, 0, 0))
    cache = pl.BlockSpec((1, wc * heads, hd), lambda b: (b, 0, 0))
    full = lambda a: pl.BlockSpec(a.shape, lambda b: (0,) * a.ndim)
    out = pl.pallas_call(
        functools.partial(_step_attn_kernel, scale=HEAD_DIM ** -0.5),
        grid=(bd,),
        in_specs=[new, new, new, cache, cache, full(bias_c), full(bias_n)],
        out_specs=new,
        out_shape=jax.ShapeDtypeStruct((bd, t * heads, hd), BF16),
        compiler_params=_cparams("parallel"),
        name="step_attn",
    )(pairs(q), pairs(k), pairs(v), k_cache.reshape(bd, wc * heads, hd),
      v_cache.reshape(bd, wc * heads, hd), bias_c, bias_n)
    return out.reshape(q.shape)


def _lru_kernel(xr_ref, gr_ref, buf0_ref, h0_ref, cw_ref, cb_ref, wgate_ref, ba_ref, bx_ref,
                lam_ref, ro_ref, hl_ref, tail_ref, xp_s, xc_s, a_s, b_s, h_s, *, tt, nb, pitch):
    nslab = a_s.shape[0]
    lane = lambda p: slice(p * LANES, (p + 1) * LANES)
    hist0 = SUBLANES - (CONV_W - 1)

    @pl.when(pl.program_id(0) == 0)
    def _():
        xp_s[:, 0:SUBLANES, :] = buf0_ref[...]
        xc_s[...] = jnp.zeros_like(xc_s)
        for p in range(nslab):
            h_s[p] = h0_ref[:, lane(p)]

    def conv(b, carry):
        xp_s[b, SUBLANES:SUBLANES + tt, :] = xr_ref[b]
        xc = cb_ref[...]
        for j in range(CONV_W):
            xc = xc + xp_s[b, hist0 + j:hist0 + j + tt, :] * cw_ref[j:j + 1, :]
        xc_s[pl.ds(pl.multiple_of(b * pitch, SUBLANES), tt), :] = xc
        xp_s[b, 0:SUBLANES, :] = xp_s[b, tt:tt + SUBLANES, :]
        return carry

    lax.fori_loop(0, nb, conv, 0)
    tail_ref[...] = xp_s[:, 0:SUBLANES, :]

    sp = jax.nn.softplus(-lam_ref[...])
    for p in range(nslab):
        xc = xc_s[:, lane(p)]
        z = _dot(xc.astype(BF16), wgate_ref[p])
        r = _sigmoid(z[:, :LANES] + ba_ref[:, lane(p)])
        ig = _sigmoid(z[:, LANES:] + bx_ref[:, lane(p)])
        log_a = -LRU_C * r * sp[:, lane(p)]
        a = jnp.exp(log_a)
        a_s[p] = a
        m2 = -jnp.tanh(log_a) * (a * a + 1.0)
        mult = jnp.where(m2 > 0.0, m2 * lax.rsqrt(m2), 0.0)
        b_s[p] = mult * (ig * xc)

    def step(t, hs):
        rows = pl.ds(t, nb, stride=pitch)
        new = []
        for p in range(nslab):
            h = a_s[p, rows, :] * hs[p] + b_s[p, rows, :]
            b_s[p, rows, :] = h
            new.append(h)
        return tuple(new)

    hs = lax.fori_loop(0, tt, step, tuple(h_s[p] for p in range(nslab)))
    for p in range(nslab):
        h_s[p] = hs[p]
        hl_ref[:, lane(p)] = hs[p]

    def gate_out(b, carry):
        r0 = pl.multiple_of(b * pitch, SUBLANES)
        for p in range(nslab):
            ro_ref[b, :, lane(p)] = (jax.nn.gelu(gr_ref[b, :, lane(p)])
                                     * b_s[p, pl.ds(r0, tt), :]).astype(ro_ref.dtype)
        return carry

    lax.fori_loop(0, nb, gate_out, 0)


def _lru(proj, conv_buf, h0, conv_w, conv_b, wgate, ba, bx, lam):
    nb, t, _ = proj.shape
    r = h0.shape[-1]
    tt = min(LRU_TIME_TILE, t)
    pitch = tt + SUBLANES
    nslab = r // LANES
    buf0 = jnp.pad(conv_buf, ((0, 0), (SUBLANES - (CONV_W - 1), 0), (0, 0)))
    blk = pl.BlockSpec((nb, tt, r), lambda i: (0, i, 0))
    xr_blk = pl.BlockSpec((nb, tt, r), lambda i: (0, i, 2))
    gr_blk = pl.BlockSpec((nb, tt, r), lambda i: (0, i, 3))
    full = lambda a: pl.BlockSpec(a.shape, lambda i: (0,) * a.ndim)
    r_out, h_last, tail = pl.pallas_call(
        functools.partial(_lru_kernel, tt=tt, nb=nb, pitch=pitch),
        grid=(t // tt,),
        in_specs=[xr_blk, gr_blk, full(buf0), full(h0), full(conv_w), full(conv_b), full(wgate),
                  full(ba), full(bx), full(lam)],
        out_specs=[blk, full(h0), full(buf0)],
        out_shape=[jax.ShapeDtypeStruct((nb, t, r), BF16),
                   jax.ShapeDtypeStruct(h0.shape, F32),
                   jax.ShapeDtypeStruct(buf0.shape, F32)],
        scratch_shapes=[pltpu.VMEM((nb, tt + SUBLANES, r), F32),
                        pltpu.VMEM((nb * pitch, r), F32),
                        pltpu.VMEM((nslab, nb * pitch, LANES), F32),
                        pltpu.VMEM((nslab, nb * pitch, LANES), F32),
                        pltpu.VMEM((nslab, nb, LANES), F32)],
        compiler_params=_cparams("arbitrary"),
        name="conv_lru",
    )(proj, proj, buf0, h0, conv_w, conv_b, wgate, ba, bx, lam)
    return r_out, h_last, tail[:, SUBLANES - (CONV_W - 1):]


def _pair_gate_weights(wa, wx):
    nblk, bs, _ = wa.shape
    def pair(w):
        w = w.reshape(nblk // 2, 2, bs, bs)
        z = jnp.zeros_like(w[:, 0])
        top = jnp.concatenate([w[:, 0], z], axis=-1)
        bot = jnp.concatenate([z, w[:, 1]], axis=-1)
        return jnp.concatenate([top, bot], axis=-2)
    return jnp.concatenate([pair(wa), pair(wx)], axis=-1).astype(BF16)


def _merge_kernel(x_ref, gn_ref, a_ref, r_ref, wga_ref, wgb_ref, wpa_ref, wpb_ref, wo_ref,
                  o_ref, xn_ref, *, nc):
    c = pl.program_id(1)

    @pl.when(c == 0)
    def _():
        xn_ref[...] = _rms(x_ref[...], gn_ref[...]).astype(BF16)
        o_ref[...] = x_ref[...]

    xn = xn_ref[...]
    m = (jax.nn.sigmoid(_dot(xn, wga_ref[...])) * _dot(a_ref[...], wpa_ref[...])
         + jax.nn.sigmoid(_dot(xn, wgb_ref[...])) * _dot(r_ref[...], wpb_ref[...]))
    o_ref[...] += _dot(m.astype(BF16), wo_ref[...])


def _merge(x, gn, a_out, r_out, w_in, w_pa, w_pb, w_out):
    n, d = x.shape
    w = a_out.shape[1]
    tm, tc = min(ROW_TILE, n), MERGE_COL_TILE
    nc = d // tc
    ga0 = (w_in.shape[1] - 2 * d) // tc
    gb0 = ga0 + nc
    row = pl.BlockSpec((tm, d), lambda i, c: (i, 0))
    a_spec = pl.BlockSpec((tm, w), lambda i, c: (i, 0))
    return pl.pallas_call(
        functools.partial(_merge_kernel, nc=nc),
        grid=(n // tm, nc),
        in_specs=[row, pl.BlockSpec((1, d), lambda i, c: (0, 0)), a_spec, a_spec,
                  pl.BlockSpec((d, tc), lambda i, c: (0, ga0 + c)),
                  pl.BlockSpec((d, tc), lambda i, c: (0, gb0 + c)),
                  pl.BlockSpec((w, tc), lambda i, c: (0, c)),
                  pl.BlockSpec((w, tc), lambda i, c: (0, c)),
                  pl.BlockSpec((tc, d), lambda i, c: (c, 0))],
        out_specs=row,
        out_shape=jax.ShapeDtypeStruct((n, d), F32),
        scratch_shapes=[pltpu.VMEM((tm, d), BF16)],
        compiler_params=_cparams("parallel", "arbitrary"),
        name="merge",
    )(x, gn, a_out, r_out, w_in, w_in, w_pa, w_pb, w_out)


def _memattn_kernel(x_ref, gn_ref, wq_ref, qg_ref, mk_ref, mv_ref, wo_ref, o_ref, mq_s, att_s,
                    *, nb, tb, scale):
    x = x_ref[...]
    _store_head_rms(_dot(_rms(x, gn_ref[...]).astype(BF16), wq_ref[...]), qg_ref[...], mq_s)
    for h in range(mq_s.shape[-1] // HEAD_DIM):
        sl = slice(h * HEAD_DIM, (h + 1) * HEAD_DIM)
        for jb in range(nb):
            rs = slice(jb * tb, (jb + 1) * tb)
            s = _dot_nt(mq_s[rs, sl], mk_ref[jb, :, sl].astype(BF16)) * scale
            p = jnp.exp(s - jnp.max(s, axis=-1, keepdims=True))
            l = jnp.sum(p, axis=-1, keepdims=True)
            o = _dot(p.astype(BF16), mv_ref[jb, :, sl].astype(BF16))
            att_s[rs, sl] = (o / l).astype(BF16)
    o_ref[...] = x + _dot(att_s[...], wo_ref[...])


def _memattn(x, gn, wq, qg, mem_k, mem_v, wo, *, batch):
    n, d = x.shape
    bm, nm, mw = mem_k.shape
    seq = n // batch
    if seq >= MEMATTN_ROW_TILE:
        tm, nb, tb = MEMATTN_ROW_TILE, 1, MEMATTN_ROW_TILE
        per = seq // tm
        mem_idx = lambda i: (i // per, 0, 0)
    else:
        tm = MEM_ROW_TILE
        nb, tb = tm // seq, seq
        mem_idx = lambda i: (i, 0, 0)
    row = pl.BlockSpec((tm, d), lambda i: (i, 0))
    full = lambda a: pl.BlockSpec(a.shape, lambda i: (0,) * a.ndim)
    mem = pl.BlockSpec((nb, nm, mw), mem_idx)
    return pl.pallas_call(
        functools.partial(_memattn_kernel, nb=nb, tb=tb, scale=HEAD_DIM ** -0.5),
        grid=(n // tm,),
        in_specs=[row, full(gn), full(wq), full(qg), mem, mem, full(wo)],
        out_specs=row,
        out_shape=jax.ShapeDtypeStruct((n, d), F32),
        scratch_shapes=[pltpu.VMEM((tm, mw), BF16), pltpu.VMEM((tm, mw), BF16)],
        compiler_params=_cparams("parallel"),
        name="memattn",
    )(x, gn, wq, qg, mem_k, mem_v, wo)


def _layer(x, p, attend, conv_buf, h0, mem_k, mem_v):
    b, t, d = x.shape
    n = b * t
    w = h0.shape[-1]
    x1 = _ffn(x.reshape(n, d), p["ffn1_norm"], p["ffn1_wg"], p["ffn1_wu"], p["ffn1_wd"])
    q, proj = _inproj(x1, p["mix_norm"], p["w_in"], p["q_norm"], p["k_norm"], w=w)
    a_out = attend(q, proj)
    r_out, h_last, new_buf = _lru(proj.reshape(b, t, 4 * w), conv_buf, h0,
                                  p["conv_w"], p["conv_b"], p["w_gate"],
                                  p["lru_ba"], p["lru_bx"], p["lru_lambda"])
    x2 = _merge(x1, p["mix_norm"], a_out, r_out.reshape(n, w), p["w_in"], p["w_proj_a"],
                p["w_proj_b"], p["w_out"])
    x3 = _memattn(x2, p["mem_norm_x"], p["w_mq"], p["mq_norm"], mem_k, mem_v, p["w_mo"], batch=b)
    y = _ffn(x3, p["ffn2_norm"], p["ffn2_wg"], p["ffn2_wu"], p["ffn2_wd"], p["final_norm"])
    kv = proj.reshape(b, t, 4 * w)
    return y.reshape(b, t, d), kv, new_buf, h_last


def kernel(x_prompt, x_sample, cache_attn_k, cache_attn_v, state_conv, state_lru, cache_mem_k, cache_mem_v, mem_prompt, ffn1_norm, ffn1_wg, ffn1_wu, ffn1_wd, mix_norm, w_in, q_norm, k_norm, rel_bias, conv_w, conv_b, lru_wa, lru_ba, lru_wx, lru_bx, lru_lambda, w_proj_a, w_proj_b, w_out, mem_norm_x, mem_norm_m, w_mq, w_mk, w_mv, mq_norm, mk_norm, w_mo, ffn2_norm, ffn2_wg, ffn2_wu, ffn2_wd, final_norm):
    depth = ffn1_norm.shape[0]
    bp, sp, d = x_prompt.shape
    bd, td, _ = x_sample.shape
    heads = rel_bias.shape[1]
    w = heads * HEAD_DIM
    mheads = w_mq.shape[-1] // HEAD_DIM
    n_mem = mem_prompt.shape[1]
    n_keep = min(BAND_ROWS, sp)
    yp, ys = x_prompt, x_sample
    outs = [[] for _ in range(10)]
    for l in range(depth):
        bf = lambda a: a[l].astype(BF16)
        half = lambda a: (0.5 * a[l]).astype(BF16)
        row = lambda a: a[l].reshape(1, -1)
        p = {
            "ffn1_norm": row(ffn1_norm), "ffn1_wg": bf(ffn1_wg), "ffn1_wu": bf(ffn1_wu), "ffn1_wd": half(ffn1_wd),
            "mix_norm": row(mix_norm), "w_in": bf(w_in), "q_norm": row(q_norm), "k_norm": row(k_norm),
            "conv_w": conv_w[l], "conv_b": row(conv_b),
            "w_gate": _pair_gate_weights(lru_wa[l], lru_wx[l]),
            "lru_ba": row(lru_ba), "lru_bx": row(lru_bx), "lru_lambda": row(lru_lambda),
            "w_proj_a": bf(w_proj_a), "w_proj_b": bf(w_proj_b), "w_out": bf(w_out),
            "mem_norm_x": row(mem_norm_x), "w_mq": bf(w_mq), "mq_norm": row(mq_norm), "w_mo": bf(w_mo),
            "ffn2_norm": row(ffn2_norm), "ffn2_wg": bf(ffn2_wg), "ffn2_wu": bf(ffn2_wu), "ffn2_wd": half(ffn2_wd),
            "final_norm": row(final_norm),
        }
        rb = rel_bias[l]
        mk, mv = _memkv(mem_prompt.reshape(bp * n_mem, d), row(mem_norm_m), bf(w_mk), bf(w_mv), row(mk_norm))
        mk, mv = mk.reshape(bp, n_mem, -1), mv.reshape(bp, n_mem, -1)
        attend_p = lambda q, proj: _band_attn(q, proj, rb, batch=bp)
        yp, kv_p, cb_p, hl_p = _layer(
            yp, p, attend_p, jnp.zeros((bp, CONV_W - 1, w), F32), jnp.zeros((bp, w), F32), mk, mv)
        heads_of = lambda a: a.reshape(a.shape[:2] + (heads, HEAD_DIM))
        kv_p = kv_p[:, sp - n_keep:, :2 * w]
        attend_s = lambda q, proj: _step_attn(q, proj[:, :w], proj[:, w:2 * w], rb,
                                              cache_attn_k[l], cache_attn_v[l])
        ys, kv_s, cb_s, hl_s = _layer(
            ys, p, attend_s, state_conv[l], state_lru[l],
            cache_mem_k[l].reshape(bd, n_mem, -1), cache_mem_v[l].reshape(bd, n_mem, -1))
        new = [heads_of(kv_p[..., :w]), heads_of(kv_p[..., w:]), cb_p, hl_p,
               mk.reshape(bp, n_mem, mheads, HEAD_DIM), mv.reshape(bp, n_mem, mheads, HEAD_DIM),
               heads_of(kv_s[..., :w]), heads_of(kv_s[..., w:2 * w]), cb_s, hl_s]
        for o, a in zip(outs, new):
            o.append(a)
    return (yp, ys) + tuple(jnp.stack(o) for o in outs)
```

```python
import functools

import numpy as np
import jax
import jax.numpy as jnp
from jax import lax
from jax.experimental import pallas as pl
from jax.experimental.pallas import tpu as pltpu

F32 = jnp.float32
BF16 = jnp.bfloat16

CHUNK = 64
LEFT_CHUNKS = 8
BAND_ROWS = LEFT_CHUNKS * CHUNK
HEAD_DIM = 128
PAST_LEN = 1024
CONV_W = 4
LRU_C = 8.0
EPS = 1e-6
NEG_INF = -1e30

VMEM_LIMIT_BYTES = 56 * 1024 * 1024
LANES = 128
SUBLANES = 8

FFN_ROW_TILE = 1024
INPROJ_ROW_TILE = 1024
ROW_TILE = 512
FFN_COL_TILE = 512
MERGE_COL_TILE = 512
ATTN_Q_TILE = 256
LRU_TIME_TILE = 128
STEP_ATTN_SEQS = 2
MEMATTN_ROW_TILE = 1024
MEM_ROW_TILE = 128


def _cparams(*sem):
    return pltpu.CompilerParams(dimension_semantics=sem, vmem_limit_bytes=VMEM_LIMIT_BYTES)


def _rms(x, g):
    ms = jnp.mean(x * x, axis=-1, keepdims=True)
    return x * lax.rsqrt(ms + EPS) * g


def _store_head_rms(u, g, o_ref):
    for h in range(u.shape[-1] // HEAD_DIM):
        sl = slice(h * HEAD_DIM, (h + 1) * HEAD_DIM)
        o_ref[:, sl] = _rms(u[:, sl], g).astype(o_ref.dtype)


def _sigmoid(x):
    return 0.5 * jnp.tanh(0.5 * x) + 0.5


def _dot(a, b):
    return jnp.dot(a, b, preferred_element_type=F32)


def _dot_nt(a, b):
    return lax.dot_general(a, b, (((1,), (1,)), ((), ())), preferred_element_type=F32)


def _ffn_kernel(*refs, nf, final_norm):
    if final_norm:
        x_ref, gn_ref, wg_ref, wu_ref, wd_ref, gf_ref, o_ref, xn_ref = refs
    else:
        x_ref, gn_ref, wg_ref, wu_ref, wd_ref, o_ref, xn_ref = refs
    f = pl.program_id(1)

    @pl.when(f == 0)
    def _():
        x = x_ref[...]
        xn_ref[...] = _rms(x, gn_ref[...]).astype(BF16)
        o_ref[...] = x

    xn = xn_ref[...]
    g = _dot(xn, wg_ref[...])
    u = _dot(xn, wu_ref[...])
    h = (jax.nn.silu(g) * u).astype(BF16)
    o_ref[...] += _dot(h, wd_ref[...])

    if final_norm:
        @pl.when(f == nf - 1)
        def _():
            o_ref[...] = _rms(o_ref[...], gf_ref[...])


def _ffn(x, gn, wg, wu, wd, final_gain=None):
    n, d = x.shape
    dff = wg.shape[1]
    tm, tf = min(FFN_ROW_TILE, n), FFN_COL_TILE
    nf = dff // tf
    final_norm = final_gain is not None
    row = pl.BlockSpec((tm, d), lambda i, f: (i, 0))
    vec = pl.BlockSpec((1, d), lambda i, f: (0, 0))
    in_specs = [row, vec,
                pl.BlockSpec((d, tf), lambda i, f: (0, f)),
                pl.BlockSpec((d, tf), lambda i, f: (0, f)),
                pl.BlockSpec((tf, d), lambda i, f: (f, 0))]
    args = [x, gn, wg, wu, wd]
    if final_norm:
        in_specs.append(vec)
        args.append(final_gain)
    return pl.pallas_call(
        functools.partial(_ffn_kernel, nf=nf, final_norm=final_norm),
        grid=(n // tm, nf),
        in_specs=in_specs,
        out_specs=row,
        out_shape=jax.ShapeDtypeStruct((n, d), F32),
        scratch_shapes=[pltpu.VMEM((tm, d), BF16)],
        compiler_params=_cparams("parallel", "arbitrary"),
        name="ffn",
    )(*args)


PROJ_K, PROJ_V, PROJ_XR, PROJ_GR = range(4)
N_PROJ_GROUPS = 5


def _inproj_kernel(x_ref, gn_ref, w_ref, qg_ref, kg_ref, q_ref, o_ref, xn_ref):
    j = pl.program_id(1)

    @pl.when(j == 0)
    def _():
        xn_ref[...] = _rms(x_ref[...], gn_ref[...]).astype(BF16)

    def normed(out_ref, g_ref):
        step = 2 * HEAD_DIM
        for c in range(0, w_ref.shape[1], step):
            u = _dot(xn_ref[...], w_ref[:, c:c + step])
            for h in range(c, c + step, HEAD_DIM):
                out_ref[:, h:h + HEAD_DIM] = _rms(u[:, h - c:h - c + HEAD_DIM], g_ref[...]).astype(out_ref.dtype)

    def plain():
        o_ref[...] = _dot(xn_ref[...], w_ref[...])

    pl.when(j == 0)(functools.partial(normed, q_ref, qg_ref))
    pl.when(j == 1)(functools.partial(normed, o_ref, kg_ref))
    pl.when(j >= 2)(plain)


def _inproj(x, gn, w_in, qg, kg, *, w):
    n, d = x.shape
    tm = min(INPROJ_ROW_TILE, n)
    row = pl.BlockSpec((tm, d), lambda i, j: (i, 0))
    vec = lambda m: pl.BlockSpec((1, m), lambda i, j: (0, 0))
    return pl.pallas_call(
        _inproj_kernel,
        grid=(n // tm, N_PROJ_GROUPS),
        in_specs=[row, vec(d), pl.BlockSpec((d, w), lambda i, j: (0, j)),
                  vec(HEAD_DIM), vec(HEAD_DIM)],
        out_specs=[pl.BlockSpec((tm, w), lambda i, j: (i, 0)),
                   pl.BlockSpec((tm, w), lambda i, j: (i, jnp.maximum(j - 1, 0))),
                   row],
        out_shape=[jax.ShapeDtypeStruct((n, w), BF16),
                   jax.ShapeDtypeStruct((n, (N_PROJ_GROUPS - 1) * w), F32),
                   jax.ShapeDtypeStruct((n, d), BF16)],
        compiler_params=_cparams("parallel", "arbitrary"),
        name="inproj",
    )(x, gn, w_in, qg, kg)


def _memkv_kernel(m_ref, gn_ref, wk_ref, wv_ref, kg_ref, k_ref, v_ref):
    xn = _rms(m_ref[...], gn_ref[...]).astype(BF16)
    _store_head_rms(_dot(xn, wk_ref[...]), kg_ref[...], k_ref)
    v_ref[...] = _dot(xn, wv_ref[...])


def _memkv(m, gn, wk, wv, kg):
    n, d = m.shape
    mw = wk.shape[1]
    tm = min(ROW_TILE, n)
    row = lambda c: pl.BlockSpec((tm, c), lambda i: (i, 0))
    full = lambda a: pl.BlockSpec(a.shape, lambda i: (0, 0))
    return pl.pallas_call(
        _memkv_kernel,
        grid=(n // tm,),
        in_specs=[row(d), full(gn), full(wk), full(wv), full(kg)],
        out_specs=[row(mw), row(mw)],
        out_shape=[jax.ShapeDtypeStruct((n, mw), F32)] * 2,
        compiler_params=_cparams("parallel"),
        name="memkv",
    )(m, gn, wk, wv, kg)


def _band_attn_kernel(q_ref, k_ref, v_ref, g_ref, vis_ref, o_ref, bias_s, kb_s, vb_s,
                      *, tq, win, nt, scale):
    nvar = vis_ref.shape[0]

    @pl.when(pl.program_id(1) == 0)
    def _():
        for var in range(nvar):
            row = jnp.broadcast_to(g_ref[0, var:var + 1, :], (tq, g_ref.shape[-1]))
            toeplitz = pltpu.roll(row, 0, 1, stride=1, stride_axis=0)
            bias_s[var] = jnp.where(vis_ref[var] > 0.5, toeplitz[:, :win], NEG_INF)

    kb_s[...] = k_ref[...].astype(BF16)
    vb_s[...] = v_ref[...].astype(BF16)

    def tile(i, carry):
        q0 = pl.multiple_of(i * tq, tq)
        ws = pl.multiple_of(jnp.maximum(i * tq - BAND_ROWS, 0), tq)
        var = jnp.minimum(i, nvar - 1)
        s = _dot_nt(q_ref[pl.ds(q0, tq), :], kb_s[pl.ds(ws, win), :]) * scale + bias_s[var]
        m = jnp.max(s, axis=-1, keepdims=True)
        p = jnp.exp(s - m)
        l = jnp.sum(p, axis=-1, keepdims=True)
        o = _dot(p.astype(BF16), vb_s[pl.ds(ws, win), :])
        o_ref[pl.ds(q0, tq), :] = (o / l).astype(o_ref.dtype)
        return carry

    lax.fori_loop(0, nt, tile, 0, unroll=4)


def _band_tables(rel_bias, tq):
    max_rel = (rel_bias.shape[-1] - 1) // 2
    win = BAND_ROWS + tq
    span = win + tq
    nvar = BAND_ROWS // tq + 1
    off = np.arange(nvar)[:, None] * tq
    c = np.arange(span)[None, :]
    rel = np.where(c < win, off - c, off + span - c)
    gen = rel_bias.astype(F32)[:, np.clip(rel, -max_rel, max_rel) + max_rel]
    qc = (off[:, :, None] + np.arange(tq)[None, :, None]) // CHUNK
    kc = np.arange(win)[None, None, :] // CHUNK
    vis = (kc <= qc) & (kc >= qc - LEFT_CHUNKS)
    return gen, jnp.asarray(vis.astype(np.float32))


def _band_attn(q, kv, rel_bias, *, batch):
    n, w = q.shape
    seq = n // batch
    heads = w // HEAD_DIM
    tq = ATTN_Q_TILE
    win = BAND_ROWS + tq
    gen, vis = _band_tables(rel_bias, tq)
    blk = pl.BlockSpec((seq, HEAD_DIM), lambda h, b: (b, h))
    k_blk = pl.BlockSpec((seq, HEAD_DIM), lambda h, b: (b, PROJ_K * heads + h))
    v_blk = pl.BlockSpec((seq, HEAD_DIM), lambda h, b: (b, PROJ_V * heads + h))
    return pl.pallas_call(
        functools.partial(_band_attn_kernel, tq=tq, win=win, nt=seq // tq, scale=HEAD_DIM ** -0.5),
        grid=(heads, batch),
        in_specs=[blk, k_blk, v_blk,
                  pl.BlockSpec((1,) + gen.shape[1:], lambda h, b: (h, 0, 0)),
                  pl.BlockSpec(vis.shape, lambda h, b: (0, 0, 0))],
        out_specs=blk,
        out_shape=jax.ShapeDtypeStruct((n, w), BF16),
        scratch_shapes=[pltpu.VMEM(vis.shape, F32), pltpu.VMEM((seq, HEAD_DIM), BF16),
                        pltpu.VMEM((seq, HEAD_DIM), BF16)],
        compiler_params=_cparams("arbitrary", "arbitrary"),
        name="band_attn",
    )(q, kv, kv, gen, vis)


def _step_attn_kernel(q_ref, kn_ref, vn_ref, kc_ref, vc_ref, bc_ref, bn_ref, o_ref, *, scale):
    for s in range(q_ref.shape[0]):
        q = q_ref[s]
        sc = _dot_nt(q, kc_ref[s].astype(BF16)) * scale + bc_ref[...]
        sn = _dot_nt(q, kn_ref[s].astype(BF16)) * scale + bn_ref[...]
        m = jnp.maximum(jnp.max(sc, axis=-1, keepdims=True), jnp.max(sn, axis=-1, keepdims=True))
        pc = jnp.exp(sc - m)
        pn = jnp.exp(sn - m)
        l = jnp.sum(pc, axis=-1, keepdims=True) + jnp.sum(pn, axis=-1, keepdims=True)
        o = (_dot(pc.astype(BF16), vc_ref[s].astype(BF16))
             + _dot(pn.astype(BF16), vn_ref[s].astype(BF16)))
        o_ref[s] = (o / l).astype(o_ref.dtype)


def _step_attn(q, k, v, rel_bias, k_cache, v_cache):
    bd, wc, heads, hd = k_cache.shape
    t = q.shape[0] // bd
    max_rel = (rel_bias.shape[-1] - 1) // 2
    qpos = PAST_LEN + np.arange(t)
    kpos = np.concatenate([PAST_LEN - wc + np.arange(wc), PAST_LEN + np.arange(t)])
    idx = np.clip(qpos[:, None] - kpos[None, :], -max_rel, max_rel) + max_rel
    qc, kc = qpos[:, None] // CHUNK, kpos[None, :] // CHUNK
    vis = (kc <= qc) & (kc >= qc - LEFT_CHUNKS) & (kpos[None, :] >= 0)
    bias = jnp.where(vis[None], rel_bias.astype(F32)[:, idx], NEG_INF)
    same_head = np.eye(heads, dtype=bool)[None, :, None, :]
    bias = jnp.where(same_head, bias.transpose(1, 0, 2)[:, :, :, None], NEG_INF)
    bias_c = bias[:, :, :wc].reshape(t * heads, wc * heads)
    bias_n = bias[:, :, wc:].reshape(t * heads, t * heads)
    pairs = lambda a: a.reshape(bd, t * heads, hd)
    ns = STEP_ATTN_SEQS if bd % STEP_ATTN_SEQS == 0 else 1
    new = pl.BlockSpec((ns, t * heads, hd), lambda b: (b, 0, 0))
    cache = pl.BlockSpec((ns, wc * heads, hd), lambda b: (b, 0, 0))
    full = lambda a: pl.BlockSpec(a.shape, lambda b: (0,) * a.ndim)
    out = pl.pallas_call(
        functools.partial(_step_attn_kernel, scale=HEAD_DIM ** -0.5),
        grid=(bd // ns,),
        in_specs=[new, new, new, cache, cache, full(bias_c), full(bias_n)],
        out_specs=new,
        out_shape=jax.ShapeDtypeStruct((bd, t * heads, hd), BF16),
        compiler_params=_cparams("parallel"),
        name="step_attn",
    )(pairs(q), pairs(k), pairs(v), k_cache.reshape(bd, wc * heads, hd),
      v_cache.reshape(bd, wc * heads, hd), bias_c, bias_n)
    return out.reshape(q.shape)


def _lru_kernel(xr_ref, gr_ref, buf0_ref, h0_ref, cw_ref, cb_ref, wgate_ref, ba_ref, bx_ref,
                lam_ref, ro_ref, hl_ref, tail_ref, xp_s, xc_s, a_s, b_s, h_s, *, tt, nb, pitch):
    nslab = a_s.shape[0]
    lane = lambda p: slice(p * LANES, (p + 1) * LANES)
    hist0 = SUBLANES - (CONV_W - 1)

    @pl.when(pl.program_id(0) == 0)
    def _():
        xp_s[:, 0:SUBLANES, :] = buf0_ref[...]
        xc_s[...] = jnp.zeros_like(xc_s)
        for p in range(nslab):
            h_s[p] = h0_ref[:, lane(p)]

    def conv(b, carry):
        xp_s[b, SUBLANES:SUBLANES + tt, :] = xr_ref[b]
        xc = cb_ref[...]
        for j in range(CONV_W):
            xc = xc + xp_s[b, hist0 + j:hist0 + j + tt, :] * cw_ref[j:j + 1, :]
        xc_s[pl.ds(pl.multiple_of(b * pitch, SUBLANES), tt), :] = xc
        xp_s[b, 0:SUBLANES, :] = xp_s[b, tt:tt + SUBLANES, :]
        return carry

    lax.fori_loop(0, nb, conv, 0)
    tail_ref[...] = xp_s[:, 0:SUBLANES, :]

    sp = jax.nn.softplus(-lam_ref[...])
    for p in range(nslab):
        xc = xc_s[:, lane(p)]
        z = _dot(xc.astype(BF16), wgate_ref[p])
        r = _sigmoid(z[:, :LANES] + ba_ref[:, lane(p)])
        ig = _sigmoid(z[:, LANES:] + bx_ref[:, lane(p)])
        log_a = -LRU_C * r * sp[:, lane(p)]
        a = jnp.exp(log_a)
        a_s[p] = a
        m2 = -jnp.tanh(log_a) * (a * a + 1.0)
        mult = jnp.where(m2 > 0.0, m2 * lax.rsqrt(m2), 0.0)
        b_s[p] = mult * (ig * xc)

    def step(t, hs):
        rows = pl.ds(t, nb, stride=pitch)
        new = []
        for p in range(nslab):
            h = a_s[p, rows, :] * hs[p] + b_s[p, rows, :]
            b_s[p, rows, :] = h
            new.append(h)
        return tuple(new)

    hs = lax.fori_loop(0, tt, step, tuple(h_s[p] for p in range(nslab)))
    for p in range(nslab):
        h_s[p] = hs[p]
        hl_ref[:, lane(p)] = hs[p]

    def gate_out(b, carry):
        r0 = pl.multiple_of(b * pitch, SUBLANES)
        for p in range(nslab):
            ro_ref[b, :, lane(p)] = (jax.nn.gelu(gr_ref[b, :, lane(p)])
                                     * b_s[p, pl.ds(r0, tt), :]).astype(ro_ref.dtype)
        return carry

    lax.fori_loop(0, nb, gate_out, 0)


def _lru(proj, conv_buf, h0, conv_w, conv_b, wgate, ba, bx, lam):
    nb, t, _ = proj.shape
    r = h0.shape[-1]
    tt = min(LRU_TIME_TILE, t)
    pitch = tt + SUBLANES
    nslab = r // LANES
    buf0 = jnp.pad(conv_buf, ((0, 0), (SUBLANES - (CONV_W - 1), 0), (0, 0)))
    blk = pl.BlockSpec((nb, tt, r), lambda i: (0, i, 0))
    xr_blk = pl.BlockSpec((nb, tt, r), lambda i: (0, i, PROJ_XR))
    gr_blk = pl.BlockSpec((nb, tt, r), lambda i: (0, i, PROJ_GR))
    full = lambda a: pl.BlockSpec(a.shape, lambda i: (0,) * a.ndim)
    r_out, h_last, tail = pl.pallas_call(
        functools.partial(_lru_kernel, tt=tt, nb=nb, pitch=pitch),
        grid=(t // tt,),
        in_specs=[xr_blk, gr_blk, full(buf0), full(h0), full(conv_w), full(conv_b), full(wgate),
                  full(ba), full(bx), full(lam)],
        out_specs=[blk, full(h0), full(buf0)],
        out_shape=[jax.ShapeDtypeStruct((nb, t, r), BF16),
                   jax.ShapeDtypeStruct(h0.shape, F32),
                   jax.ShapeDtypeStruct(buf0.shape, F32)],
        scratch_shapes=[pltpu.VMEM((nb, tt + SUBLANES, r), F32),
                        pltpu.VMEM((nb * pitch, r), F32),
                        pltpu.VMEM((nslab, nb * pitch, LANES), F32),
                        pltpu.VMEM((nslab, nb * pitch, LANES), F32),
                        pltpu.VMEM((nslab, nb, LANES), F32)],
        compiler_params=_cparams("arbitrary"),
        name="conv_lru",
    )(proj, proj, buf0, h0, conv_w, conv_b, wgate, ba, bx, lam)
    return r_out, h_last, tail[:, SUBLANES - (CONV_W - 1):]


def _pair_gate_weights(wa, wx):
    nblk, bs, _ = wa.shape
    def pair(w):
        w = w.reshape(nblk // 2, 2, bs, bs)
        z = jnp.zeros_like(w[:, 0])
        top = jnp.concatenate([w[:, 0], z], axis=-1)
        bot = jnp.concatenate([z, w[:, 1]], axis=-1)
        return jnp.concatenate([top, bot], axis=-2)
    return jnp.concatenate([pair(wa), pair(wx)], axis=-1).astype(BF16)


def _merge_kernel(x_ref, xn_ref, a_ref, r_ref, wga_ref, wgb_ref, wpa_ref, wpb_ref, wo_ref, o_ref):
    @pl.when(pl.program_id(1) == 0)
    def _():
        o_ref[...] = x_ref[...]

    xn = xn_ref[...]
    m = (jax.nn.sigmoid(_dot(xn, wga_ref[...])) * _dot(a_ref[...], wpa_ref[...])
         + jax.nn.sigmoid(_dot(xn, wgb_ref[...])) * _dot(r_ref[...], wpb_ref[...]))
    o_ref[...] += _dot(m.astype(BF16), wo_ref[...])


def _merge(x, xn, a_out, r_out, w_in, w_pa, w_pb, w_out):
    n, d = x.shape
    w = a_out.shape[1]
    tm, tc = min(ROW_TILE, n), MERGE_COL_TILE
    nc = d // tc
    ga0 = (w_in.shape[1] - 2 * d) // tc
    gb0 = ga0 + nc
    row = pl.BlockSpec((tm, d), lambda i, c: (i, 0))
    a_spec = pl.BlockSpec((tm, w), lambda i, c: (i, 0))
    return pl.pallas_call(
        _merge_kernel,
        grid=(n // tm, nc),
        in_specs=[row, row, a_spec, a_spec,
                  pl.BlockSpec((d, tc), lambda i, c: (0, ga0 + c)),
                  pl.BlockSpec((d, tc), lambda i, c: (0, gb0 + c)),
                  pl.BlockSpec((w, tc), lambda i, c: (0, c)),
                  pl.BlockSpec((w, tc), lambda i, c: (0, c)),
                  pl.BlockSpec((tc, d), lambda i, c: (c, 0))],
        out_specs=row,
        out_shape=jax.ShapeDtypeStruct((n, d), F32),
        compiler_params=_cparams("parallel", "arbitrary"),
        name="merge",
    )(x, xn, a_out, r_out, w_in, w_in, w_pa, w_pb, w_out)


def _memattn_kernel(x_ref, gn_ref, wq_ref, qg_ref, mk_ref, mv_ref, wo_ref, o_ref, mq_s, att_s,
                    *, nb, tb, scale):
    x = x_ref[...]
    _store_head_rms(_dot(_rms(x, gn_ref[...]).astype(BF16), wq_ref[...]), qg_ref[...], mq_s)
    for h in range(mq_s.shape[-1] // HEAD_DIM):
        sl = slice(h * HEAD_DIM, (h + 1) * HEAD_DIM)
        for jb in range(nb):
            rs = slice(jb * tb, (jb + 1) * tb)
            s = _dot_nt(mq_s[rs, sl], mk_ref[jb, :, sl].astype(BF16)) * scale
            p = jnp.exp(s - jnp.max(s, axis=-1, keepdims=True))
            l = jnp.sum(p, axis=-1, keepdims=True)
            o = _dot(p.astype(BF16), mv_ref[jb, :, sl].astype(BF16))
            att_s[rs, sl] = (o / l).astype(BF16)
    o_ref[...] = x + _dot(att_s[...], wo_ref[...])


def _memattn(x, gn, wq, qg, mem_k, mem_v, wo, *, batch):
    n, d = x.shape
    bm, nm, mw = mem_k.shape
    seq = n // batch
    if seq >= MEMATTN_ROW_TILE:
        tm, nb, tb = MEMATTN_ROW_TILE, 1, MEMATTN_ROW_TILE
        per = seq // tm
        mem_idx = lambda i: (i // per, 0, 0)
    else:
        tm = MEM_ROW_TILE
        nb, tb = tm // seq, seq
        mem_idx = lambda i: (i, 0, 0)
    row = pl.BlockSpec((tm, d), lambda i: (i, 0))
    full = lambda a: pl.BlockSpec(a.shape, lambda i: (0,) * a.ndim)
    mem = pl.BlockSpec((nb, nm, mw), mem_idx)
    return pl.pallas_call(
        functools.partial(_memattn_kernel, nb=nb, tb=tb, scale=HEAD_DIM ** -0.5),
        grid=(n // tm,),
        in_specs=[row, full(gn), full(wq), full(qg), mem, mem, full(wo)],
        out_specs=row,
        out_shape=jax.ShapeDtypeStruct((n, d), F32),
        scratch_shapes=[pltpu.VMEM((tm, mw), BF16), pltpu.VMEM((tm, mw), BF16)],
        compiler_params=_cparams("parallel"),
        name="memattn",
    )(x, gn, wq, qg, mem_k, mem_v, wo)


def _layer(x, p, attend, conv_buf, h0, mem_k, mem_v):
    b, t, d = x.shape
    n = b * t
    w = h0.shape[-1]
    x1 = _ffn(x.reshape(n, d), p["ffn1_norm"], p["ffn1_wg"], p["ffn1_wu"], p["ffn1_wd"])
    q, proj, xn1 = _inproj(x1, p["mix_norm"], p["w_in"], p["q_norm"], p["k_norm"], w=w)
    a_out = attend(q, proj)
    proj3 = proj.reshape(b, t, -1)
    r_out, h_last, new_buf = _lru(proj3, conv_buf, h0, p["conv_w"], p["conv_b"], p["w_gate"],
                                  p["lru_ba"], p["lru_bx"], p["lru_lambda"])
    x2 = _merge(x1, xn1, a_out, r_out.reshape(n, w), p["w_in"], p["w_proj_a"],
                p["w_proj_b"], p["w_out"])
    x3 = _memattn(x2, p["mem_norm_x"], p["w_mq"], p["mq_norm"], mem_k, mem_v, p["w_mo"], batch=b)
    y = _ffn(x3, p["ffn2_norm"], p["ffn2_wg"], p["ffn2_wu"], p["ffn2_wd"], p["final_norm"])
    group = lambda g: proj3[..., g * w:(g + 1) * w]
    return y.reshape(b, t, d), group(PROJ_K), group(PROJ_V), new_buf, h_last


def kernel(x_prompt, x_sample, cache_attn_k, cache_attn_v, state_conv, state_lru, cache_mem_k, cache_mem_v, mem_prompt, ffn1_norm, ffn1_wg, ffn1_wu, ffn1_wd, mix_norm, w_in, q_norm, k_norm, rel_bias, conv_w, conv_b, lru_wa, lru_ba, lru_wx, lru_bx, lru_lambda, w_proj_a, w_proj_b, w_out, mem_norm_x, mem_norm_m, w_mq, w_mk, w_mv, mq_norm, mk_norm, w_mo, ffn2_norm, ffn2_wg, ffn2_wu, ffn2_wd, final_norm):
    depth = ffn1_norm.shape[0]
    bp, sp, d = x_prompt.shape
    bd, td, _ = x_sample.shape
    heads = rel_bias.shape[1]
    w = heads * HEAD_DIM
    mheads = w_mq.shape[-1] // HEAD_DIM
    n_mem = mem_prompt.shape[1]
    n_keep = min(BAND_ROWS, sp)
    yp, ys = x_prompt, x_sample
    outs = [[] for _ in range(10)]
    for l in range(depth):
        bf = lambda a: a[l].astype(BF16)
        half = lambda a: (0.5 * a[l]).astype(BF16)
        row = lambda a: a[l].reshape(1, -1)
        p = {
            "ffn1_norm": row(ffn1_norm), "ffn1_wg": bf(ffn1_wg), "ffn1_wu": bf(ffn1_wu), "ffn1_wd": half(ffn1_wd),
            "mix_norm": row(mix_norm), "w_in": bf(w_in), "q_norm": row(q_norm), "k_norm": row(k_norm),
            "conv_w": conv_w[l], "conv_b": row(conv_b),
            "w_gate": _pair_gate_weights(lru_wa[l], lru_wx[l]),
            "lru_ba": row(lru_ba), "lru_bx": row(lru_bx), "lru_lambda": row(lru_lambda),
            "w_proj_a": bf(w_proj_a), "w_proj_b": bf(w_proj_b), "w_out": bf(w_out),
            "mem_norm_x": row(mem_norm_x), "w_mq": bf(w_mq), "mq_norm": row(mq_norm), "w_mo": bf(w_mo),
            "ffn2_norm": row(ffn2_norm), "ffn2_wg": bf(ffn2_wg), "ffn2_wu": bf(ffn2_wu), "ffn2_wd": half(ffn2_wd),
            "final_norm": row(final_norm),
        }
        rb = rel_bias[l]
        mk, mv = _memkv(mem_prompt.reshape(bp * n_mem, d), row(mem_norm_m), bf(w_mk), bf(w_mv), row(mk_norm))
        mk, mv = mk.reshape(bp, n_mem, -1), mv.reshape(bp, n_mem, -1)
        attend_p = lambda q, proj: _band_attn(q, proj, rb, batch=bp)
        yp, k_p, v_p, cb_p, hl_p = _layer(
            yp, p, attend_p, jnp.zeros((bp, CONV_W - 1, w), F32), jnp.zeros((bp, w), F32), mk, mv)
        heads_of = lambda a: a.reshape(a.shape[:2] + (heads, HEAD_DIM))
        keep = lambda a: heads_of(a[:, sp - n_keep:])
        attend_s = lambda q, proj: _step_attn(q, proj[:, PROJ_K * w:(PROJ_K + 1) * w],
                                              proj[:, PROJ_V * w:(PROJ_V + 1) * w], rb,
                                              cache_attn_k[l], cache_attn_v[l])
        ys, k_s, v_s, cb_s, hl_s = _layer(
            ys, p, attend_s, state_conv[l], state_lru[l],
            cache_mem_k[l].reshape(bd, n_mem, -1), cache_mem_v[l].reshape(bd, n_mem, -1))
        new = [keep(k_p), keep(v_p), cb_p, hl_p,
               mk.reshape(bp, n_mem, mheads, HEAD_DIM), mv.reshape(bp, n_mem, mheads, HEAD_DIM),
               heads_of(k_s), heads_of(v_s), cb_s, hl_s]
        for o, a in zip(outs, new):
            o.append(a)
    return (yp, ys) + tuple(jnp.stack(o) for o in outs)
```

```python
import functools

import numpy as np
import jax
import jax.numpy as jnp
from jax import lax
from jax.experimental import pallas as pl
from jax.experimental.pallas import tpu as pltpu

F32 = jnp.float32
BF16 = jnp.bfloat16

CHUNK = 64
LEFT_CHUNKS = 8
BAND_ROWS = LEFT_CHUNKS * CHUNK
HEAD_DIM = 128
PAST_LEN = 1024
CONV_W = 4
LRU_C = 8.0
EPS = 1e-6
NEG_INF = -1e30

VMEM_LIMIT_BYTES = 56 * 1024 * 1024
LANES = 128
SUBLANES = 8

FFN_ROW_TILE = 1024
INPROJ_ROW_TILE = 1024
ROW_TILE = 512
FFN_COL_TILE = 512
FFN_NORM_BLOCKS = 4
MERGE_COL_TILE = 512
ATTN_Q_TILE = 256
LRU_TIME_TILE = 128
STEP_ATTN_SEQS = 2
MEMATTN_ROW_TILE = 1024
MEM_ROW_TILE = 128


def _cparams(*sem):
    return pltpu.CompilerParams(dimension_semantics=sem, vmem_limit_bytes=VMEM_LIMIT_BYTES)


def _rms(x, g):
    ms = jnp.mean(x * x, axis=-1, keepdims=True)
    return x * lax.rsqrt(ms + EPS) * g


def _store_head_rms(u, g, o_ref):
    for h in range(u.shape[-1] // HEAD_DIM):
        sl = slice(h * HEAD_DIM, (h + 1) * HEAD_DIM)
        o_ref[:, sl] = _rms(u[:, sl], g).astype(o_ref.dtype)


def _sigmoid(x):
    return 0.5 * jnp.tanh(0.5 * x) + 0.5


def _dot(a, b):
    return jnp.dot(a, b, preferred_element_type=F32)


def _dot_nt(a, b):
    return lax.dot_general(a, b, (((1,), (1,)), ((), ())), preferred_element_type=F32)


def _ffn_kernel(*refs, nf, final_norm):
    if final_norm:
        x_ref, gn_ref, wg_ref, wu_ref, wd_ref, gf_ref, o_ref, xn_ref = refs
    else:
        x_ref, gn_ref, wg_ref, wu_ref, wd_ref, o_ref, xn_ref = refs
    f = pl.program_id(1)

    def chunk(xn):
        h = (jax.nn.silu(_dot(xn, wg_ref[...])) * _dot(xn, wu_ref[...])).astype(BF16)
        return _dot(h, wd_ref[...])

    rb = x_ref.shape[0] // FFN_NORM_BLOCKS
    blocks = [slice(r, r + rb) for r in range(0, x_ref.shape[0], rb)]

    @pl.when(f == 0)
    def _():
        for rows in blocks:
            x = x_ref[rows, :]
            xn = _rms(x, gn_ref[...]).astype(BF16)
            xn_ref[rows, :] = xn
            o_ref[rows, :] = x + chunk(xn)

    @pl.when((f > 0) & (f < nf - 1) if final_norm else f > 0)
    def _():
        o_ref[...] += chunk(xn_ref[...])

    if final_norm:
        @pl.when(f == nf - 1)
        def _():
            for rows in blocks:
                o_ref[rows, :] = _rms(o_ref[rows, :] + chunk(xn_ref[rows, :]), gf_ref[...])


def _ffn(x, gn, wg, wu, wd, final_gain=None):
    n, d = x.shape
    dff = wg.shape[1]
    tm, tf = min(FFN_ROW_TILE, n), FFN_COL_TILE
    nf = dff // tf
    assert nf >= 2, "first and last d_ff steps are distinct code paths"
    final_norm = final_gain is not None
    row = pl.BlockSpec((tm, d), lambda i, f: (i, 0))
    vec = pl.BlockSpec((1, d), lambda i, f: (0, 0))
    in_specs = [row, vec,
                pl.BlockSpec((d, tf), lambda i, f: (0, f)),
                pl.BlockSpec((d, tf), lambda i, f: (0, f)),
                pl.BlockSpec((tf, d), lambda i, f: (f, 0))]
    args = [x, gn, wg, wu, wd]
    if final_norm:
        in_specs.append(vec)
        args.append(final_gain)
    return pl.pallas_call(
        functools.partial(_ffn_kernel, nf=nf, final_norm=final_norm),
        grid=(n // tm, nf),
        in_specs=in_specs,
        out_specs=row,
        out_shape=jax.ShapeDtypeStruct((n, d), F32),
        scratch_shapes=[pltpu.VMEM((tm, d), BF16)],
        compiler_params=_cparams("parallel", "arbitrary"),
        name="ffn",
    )(*args)


PROJ_K, PROJ_V, PROJ_XR, PROJ_GR = range(4)
N_PROJ_GROUPS = 5


def _inproj_kernel(x_ref, gn_ref, w_ref, qg_ref, kg_ref, q_ref, o_ref, xn_ref):
    j = pl.program_id(1)

    @pl.when(j == 0)
    def _():
        xn_ref[...] = _rms(x_ref[...], gn_ref[...]).astype(BF16)

    def normed(out_ref, g_ref):
        step = 2 * HEAD_DIM
        for c in range(0, w_ref.shape[1], step):
            u = _dot(xn_ref[...], w_ref[:, c:c + step])
            for h in range(c, c + step, HEAD_DIM):
                out_ref[:, h:h + HEAD_DIM] = _rms(u[:, h - c:h - c + HEAD_DIM], g_ref[...]).astype(out_ref.dtype)

    def plain():
        o_ref[...] = _dot(xn_ref[...], w_ref[...])

    pl.when(j == 0)(functools.partial(normed, q_ref, qg_ref))
    pl.when(j == 1)(functools.partial(normed, o_ref, kg_ref))
    pl.when(j >= 2)(plain)


def _inproj(x, gn, w_in, qg, kg, *, w):
    n, d = x.shape
    tm = min(INPROJ_ROW_TILE, n)
    row = pl.BlockSpec((tm, d), lambda i, j: (i, 0))
    vec = lambda m: pl.BlockSpec((1, m), lambda i, j: (0, 0))
    return pl.pallas_call(
        _inproj_kernel,
        grid=(n // tm, N_PROJ_GROUPS),
        in_specs=[row, vec(d), pl.BlockSpec((d, w), lambda i, j: (0, j)),
                  vec(HEAD_DIM), vec(HEAD_DIM)],
        out_specs=[pl.BlockSpec((tm, w), lambda i, j: (i, 0)),
                   pl.BlockSpec((tm, w), lambda i, j: (i, jnp.maximum(j - 1, 0))),
                   row],
        out_shape=[jax.ShapeDtypeStruct((n, w), BF16),
                   jax.ShapeDtypeStruct((n, (N_PROJ_GROUPS - 1) * w), F32),
                   jax.ShapeDtypeStruct((n, d), BF16)],
        compiler_params=_cparams("parallel", "arbitrary"),
        name="inproj",
    )(x, gn, w_in, qg, kg)


def _memkv_kernel(m_ref, gn_ref, wk_ref, wv_ref, kg_ref, k_ref, v_ref):
    xn = _rms(m_ref[...], gn_ref[...]).astype(BF16)
    _store_head_rms(_dot(xn, wk_ref[...]), kg_ref[...], k_ref)
    v_ref[...] = _dot(xn, wv_ref[...])


def _memkv(m, gn, wk, wv, kg):
    n, d = m.shape
    mw = wk.shape[1]
    tm = min(ROW_TILE, n)
    row = lambda c: pl.BlockSpec((tm, c), lambda i: (i, 0))
    full = lambda a: pl.BlockSpec(a.shape, lambda i: (0, 0))
    return pl.pallas_call(
        _memkv_kernel,
        grid=(n // tm,),
        in_specs=[row(d), full(gn), full(wk), full(wv), full(kg)],
        out_specs=[row(mw), row(mw)],
        out_shape=[jax.ShapeDtypeStruct((n, mw), F32)] * 2,
        compiler_params=_cparams("parallel"),
        name="memkv",
    )(m, gn, wk, wv, kg)


def _band_attn_kernel(q_ref, k_ref, v_ref, g_ref, vis_ref, o_ref, bias_s, kb_s, vb_s,
                      *, tq, win, nt, scale):
    nvar = vis_ref.shape[0]

    @pl.when(pl.program_id(1) == 0)
    def _():
        for var in range(nvar):
            row = jnp.broadcast_to(g_ref[0, var:var + 1, :], (tq, g_ref.shape[-1]))
            toeplitz = pltpu.roll(row, 0, 1, stride=1, stride_axis=0)
            bias_s[var] = jnp.where(vis_ref[var] > 0.5, toeplitz[:, :win], NEG_INF)

    kb_s[...] = k_ref[...].astype(BF16)
    vb_s[...] = v_ref[...].astype(BF16)

    def tile(i, carry):
        q0 = pl.multiple_of(i * tq, tq)
        ws = pl.multiple_of(jnp.maximum(i * tq - BAND_ROWS, 0), tq)
        var = jnp.minimum(i, nvar - 1)
        s = _dot_nt(q_ref[pl.ds(q0, tq), :], kb_s[pl.ds(ws, win), :]) * scale + bias_s[var]
        m = jnp.max(s, axis=-1, keepdims=True)
        p = jnp.exp(s - m)
        l = jnp.sum(p, axis=-1, keepdims=True)
        o = _dot(p.astype(BF16), vb_s[pl.ds(ws, win), :])
        o_ref[pl.ds(q0, tq), :] = (o / l).astype(o_ref.dtype)
        return carry

    lax.fori_loop(0, nt, tile, 0, unroll=4)


def _band_tables(rel_bias, tq):
    max_rel = (rel_bias.shape[-1] - 1) // 2
    win = BAND_ROWS + tq
    span = win + tq
    nvar = BAND_ROWS // tq + 1
    off = np.arange(nvar)[:, None] * tq
    c = np.arange(span)[None, :]
    rel = np.where(c < win, off - c, off + span - c)
    gen = rel_bias.astype(F32)[:, np.clip(rel, -max_rel, max_rel) + max_rel]
    qc = (off[:, :, None] + np.arange(tq)[None, :, None]) // CHUNK
    kc = np.arange(win)[None, None, :] // CHUNK
    vis = (kc <= qc) & (kc >= qc - LEFT_CHUNKS)
    return gen, jnp.asarray(vis.astype(np.float32))


def _band_attn(q, kv, rel_bias, *, batch):
    n, w = q.shape
    seq = n // batch
    heads = w // HEAD_DIM
    tq = ATTN_Q_TILE
    win = BAND_ROWS + tq
    gen, vis = _band_tables(rel_bias, tq)
    blk = pl.BlockSpec((seq, HEAD_DIM), lambda h, b: (b, h))
    k_blk = pl.BlockSpec((seq, HEAD_DIM), lambda h, b: (b, PROJ_K * heads + h))
    v_blk = pl.BlockSpec((seq, HEAD_DIM), lambda h, b: (b, PROJ_V * heads + h))
    return pl.pallas_call(
        functools.partial(_band_attn_kernel, tq=tq, win=win, nt=seq // tq, scale=HEAD_DIM ** -0.5),
        grid=(heads, batch),
        in_specs=[blk, k_blk, v_blk,
                  pl.BlockSpec((1,) + gen.shape[1:], lambda h, b: (h, 0, 0)),
                  pl.BlockSpec(vis.shape, lambda h, b: (0, 0, 0))],
        out_specs=blk,
        out_shape=jax.ShapeDtypeStruct((n, w), BF16),
        scratch_shapes=[pltpu.VMEM(vis.shape, F32), pltpu.VMEM((seq, HEAD_DIM), BF16),
                        pltpu.VMEM((seq, HEAD_DIM), BF16)],
        compiler_params=_cparams("arbitrary", "arbitrary"),
        name="band_attn",
    )(q, kv, kv, gen, vis)


def _step_attn_kernel(q_ref, kn_ref, vn_ref, kc_ref, vc_ref, bc_ref, bn_ref, o_ref, *, scale):
    for s in range(q_ref.shape[0]):
        q = q_ref[s]
        sc = _dot_nt(q, kc_ref[s].astype(BF16)) * scale + bc_ref[...]
        sn = _dot_nt(q, kn_ref[s].astype(BF16)) * scale + bn_ref[...]
        m = jnp.maximum(jnp.max(sc, axis=-1, keepdims=True), jnp.max(sn, axis=-1, keepdims=True))
        pc = jnp.exp(sc - m)
        pn = jnp.exp(sn - m)
        l = jnp.sum(pc, axis=-1, keepdims=True) + jnp.sum(pn, axis=-1, keepdims=True)
        o = (_dot(pc.astype(BF16), vc_ref[s].astype(BF16))
             + _dot(pn.astype(BF16), vn_ref[s].astype(BF16)))
        o_ref[s] = (o / l).astype(o_ref.dtype)


def _step_attn(q, k, v, rel_bias, k_cache, v_cache):
    bd, wc, heads, hd = k_cache.shape
    t = q.shape[0] // bd
    max_rel = (rel_bias.shape[-1] - 1) // 2
    qpos = PAST_LEN + np.arange(t)
    kpos = np.concatenate([PAST_LEN - wc + np.arange(wc), PAST_LEN + np.arange(t)])
    idx = np.clip(qpos[:, None] - kpos[None, :], -max_rel, max_rel) + max_rel
    qc, kc = qpos[:, None] // CHUNK, kpos[None, :] // CHUNK
    vis = (kc <= qc) & (kc >= qc - LEFT_CHUNKS) & (kpos[None, :] >= 0)
    bias = jnp.where(vis[None], rel_bias.astype(F32)[:, idx], NEG_INF)
    same_head = np.eye(heads, dtype=bool)[None, :, None, :]
    bias = jnp.where(same_head, bias.transpose(1, 0, 2)[:, :, :, None], NEG_INF)
    bias_c = bias[:, :, :wc].reshape(t * heads, wc * heads)
    bias_n = bias[:, :, wc:].reshape(t * heads, t * heads)
    pairs = lambda a: a.reshape(bd, t * heads, hd)
    ns = STEP_ATTN_SEQS if bd % STEP_ATTN_SEQS == 0 else 1
    new = pl.BlockSpec((ns, t * heads, hd), lambda b: (b, 0, 0))
    cache = pl.BlockSpec((ns, wc * heads, hd), lambda b: (b, 0, 0))
    full = lambda a: pl.BlockSpec(a.shape, lambda b: (0,) * a.ndim)
    out = pl.pallas_call(
        functools.partial(_step_attn_kernel, scale=HEAD_DIM ** -0.5),
        grid=(bd // ns,),
        in_specs=[new, new, new, cache, cache, full(bias_c), full(bias_n)],
        out_specs=new,
        out_shape=jax.ShapeDtypeStruct((bd, t * heads, hd), BF16),
        compiler_params=_cparams("parallel"),
        name="step_attn",
    )(pairs(q), pairs(k), pairs(v), k_cache.reshape(bd, wc * heads, hd),
      v_cache.reshape(bd, wc * heads, hd), bias_c, bias_n)
    return out.reshape(q.shape)


def _lru_kernel(xr_ref, gr_ref, buf0_ref, h0_ref, cw_ref, cb_ref, wgate_ref, ba_ref, bx_ref,
                lam_ref, ro_ref, hl_ref, tail_ref, xp_s, xc_s, a_s, b_s, h_s, *, tt, nb, pitch):
    nslab = a_s.shape[0]
    lane = lambda p: slice(p * LANES, (p + 1) * LANES)
    hist0 = SUBLANES - (CONV_W - 1)

    @pl.when(pl.program_id(0) == 0)
    def _():
        xp_s[:, 0:SUBLANES, :] = buf0_ref[...]
        xc_s[...] = jnp.zeros_like(xc_s)
        for p in range(nslab):
            h_s[p] = h0_ref[:, lane(p)]

    def conv(b, carry):
        xp_s[b, SUBLANES:SUBLANES + tt, :] = xr_ref[b]
        xc = cb_ref[...]
        for j in range(CONV_W):
            xc = xc + xp_s[b, hist0 + j:hist0 + j + tt, :] * cw_ref[j:j + 1, :]
        xc_s[pl.ds(pl.multiple_of(b * pitch, SUBLANES), tt), :] = xc
        xp_s[b, 0:SUBLANES, :] = xp_s[b, tt:tt + SUBLANES, :]
        return carry

    lax.fori_loop(0, nb, conv, 0)
    tail_ref[...] = xp_s[:, 0:SUBLANES, :]

    sp = jax.nn.softplus(-lam_ref[...])
    for p in range(nslab):
        xc = xc_s[:, lane(p)]
        z = _dot(xc.astype(BF16), wgate_ref[p])
        r = _sigmoid(z[:, :LANES] + ba_ref[:, lane(p)])
        ig = _sigmoid(z[:, LANES:] + bx_ref[:, lane(p)])
        log_a = -LRU_C * r * sp[:, lane(p)]
        a = jnp.exp(log_a)
        a_s[p] = a
        m2 = -jnp.tanh(log_a) * (a * a + 1.0)
        mult = jnp.where(m2 > 0.0, m2 * lax.rsqrt(m2), 0.0)
        b_s[p] = mult * (ig * xc)

    def step(t, hs):
        rows = pl.ds(t, nb, stride=pitch)
        new = []
        for p in range(nslab):
            h = a_s[p, rows, :] * hs[p] + b_s[p, rows, :]
            b_s[p, rows, :] = h
            new.append(h)
        return tuple(new)

    hs = lax.fori_loop(0, tt, step, tuple(h_s[p] for p in range(nslab)))
    for p in range(nslab):
        h_s[p] = hs[p]
        hl_ref[:, lane(p)] = hs[p]

    def gate_out(b, carry):
        r0 = pl.multiple_of(b * pitch, SUBLANES)
        for p in range(nslab):
            ro_ref[b, :, lane(p)] = (jax.nn.gelu(gr_ref[b, :, lane(p)])
                                     * b_s[p, pl.ds(r0, tt), :]).astype(ro_ref.dtype)
        return carry

    lax.fori_loop(0, nb, gate_out, 0)


def _lru(proj, conv_buf, h0, conv_w, conv_b, wgate, ba, bx, lam):
    nb, t, _ = proj.shape
    r = h0.shape[-1]
    tt = min(LRU_TIME_TILE, t)
    pitch = tt + SUBLANES
    nslab = r // LANES
    buf0 = jnp.pad(conv_buf, ((0, 0), (SUBLANES - (CONV_W - 1), 0), (0, 0)))
    blk = pl.BlockSpec((nb, tt, r), lambda i: (0, i, 0))
    xr_blk = pl.BlockSpec((nb, tt, r), lambda i: (0, i, PROJ_XR))
    gr_blk = pl.BlockSpec((nb, tt, r), lambda i: (0, i, PROJ_GR))
    full = lambda a: pl.BlockSpec(a.shape, lambda i: (0,) * a.ndim)
    r_out, h_last, tail = pl.pallas_call(
        functools.partial(_lru_kernel, tt=tt, nb=nb, pitch=pitch),
        grid=(t // tt,),
        in_specs=[xr_blk, gr_blk, full(buf0), full(h0), full(conv_w), full(conv_b), full(wgate),
                  full(ba), full(bx), full(lam)],
        out_specs=[blk, full(h0), full(buf0)],
        out_shape=[jax.ShapeDtypeStruct((nb, t, r), BF16),
                   jax.ShapeDtypeStruct(h0.shape, F32),
                   jax.ShapeDtypeStruct(buf0.shape, F32)],
        scratch_shapes=[pltpu.VMEM((nb, tt + SUBLANES, r), F32),
                        pltpu.VMEM((nb * pitch, r), F32),
                        pltpu.VMEM((nslab, nb * pitch, LANES), F32),
                        pltpu.VMEM((nslab, nb * pitch, LANES), F32),
                        pltpu.VMEM((nslab, nb, LANES), F32)],
        compiler_params=_cparams("arbitrary"),
        name="conv_lru",
    )(proj, proj, buf0, h0, conv_w, conv_b, wgate, ba, bx, lam)
    return r_out, h_last, tail[:, SUBLANES - (CONV_W - 1):]


def _pair_gate_weights(wa, wx):
    nblk, bs, _ = wa.shape
    def pair(w):
        w = w.reshape(nblk // 2, 2, bs, bs)
        z = jnp.zeros_like(w[:, 0])
        top = jnp.concatenate([w[:, 0], z], axis=-1)
        bot = jnp.concatenate([z, w[:, 1]], axis=-1)
        return jnp.concatenate([top, bot], axis=-2)
    return jnp.concatenate([pair(wa), pair(wx)], axis=-1).astype(BF16)


def _merge_kernel(x_ref, xn_ref, a_ref, r_ref, wga_ref, wgb_ref, wpa_ref, wpb_ref, wo_ref, o_ref):
    @pl.when(pl.program_id(1) == 0)
    def _():
        o_ref[...] = x_ref[...]

    xn = xn_ref[...]
    m = (jax.nn.sigmoid(_dot(xn, wga_ref[...])) * _dot(a_ref[...], wpa_ref[...])
         + jax.nn.sigmoid(_dot(xn, wgb_ref[...])) * _dot(r_ref[...], wpb_ref[...]))
    o_ref[...] += _dot(m.astype(BF16), wo_ref[...])


def _merge(x, xn, a_out, r_out, w_in, w_pa, w_pb, w_out):
    n, d = x.shape
    w = a_out.shape[1]
    tm, tc = min(ROW_TILE, n), MERGE_COL_TILE
    nc = d // tc
    ga0 = (w_in.shape[1] - 2 * d) // tc
    gb0 = ga0 + nc
    row = pl.BlockSpec((tm, d), lambda i, c: (i, 0))
    a_spec = pl.BlockSpec((tm, w), lambda i, c: (i, 0))
    return pl.pallas_call(
        _merge_kernel,
        grid=(n // tm, nc),
        in_specs=[row, row, a_spec, a_spec,
                  pl.BlockSpec((d, tc), lambda i, c: (0, ga0 + c)),
                  pl.BlockSpec((d, tc), lambda i, c: (0, gb0 + c)),
                  pl.BlockSpec((w, tc), lambda i, c: (0, c)),
                  pl.BlockSpec((w, tc), lambda i, c: (0, c)),
                  pl.BlockSpec((tc, d), lambda i, c: (c, 0))],
        out_specs=row,
        out_shape=jax.ShapeDtypeStruct((n, d), F32),
        compiler_params=_cparams("parallel", "arbitrary"),
        name="merge",
    )(x, xn, a_out, r_out, w_in, w_in, w_pa, w_pb, w_out)


def _memattn_kernel(x_ref, gn_ref, wq_ref, qg_ref, mk_ref, mv_ref, wo_ref, o_ref, mq_s, att_s,
                    *, nb, tb, scale):
    x = x_ref[...]
    _store_head_rms(_dot(_rms(x, gn_ref[...]).astype(BF16), wq_ref[...]), qg_ref[...], mq_s)
    for h in range(mq_s.shape[-1] // HEAD_DIM):
        sl = slice(h * HEAD_DIM, (h + 1) * HEAD_DIM)
        for jb in range(nb):
            rs = slice(jb * tb, (jb + 1) * tb)
            s = _dot_nt(mq_s[rs, sl], mk_ref[jb, :, sl].astype(BF16)) * scale
            p = jnp.exp(s - jnp.max(s, axis=-1, keepdims=True))
            l = jnp.sum(p, axis=-1, keepdims=True)
            o = _dot(p.astype(BF16), mv_ref[jb, :, sl].astype(BF16))
            att_s[rs, sl] = (o / l).astype(BF16)
    o_ref[...] = x + _dot(att_s[...], wo_ref[...])


def _memattn(x, gn, wq, qg, mem_k, mem_v, wo, *, batch):
    n, d = x.shape
    bm, nm, mw = mem_k.shape
    seq = n // batch
    if seq >= MEMATTN_ROW_TILE:
        tm, nb, tb = MEMATTN_ROW_TILE, 1, MEMATTN_ROW_TILE
        per = seq // tm
        mem_idx = lambda i: (i // per, 0, 0)
    else:
        tm = MEM_ROW_TILE
        nb, tb = tm // seq, seq
        mem_idx = lambda i: (i, 0, 0)
    row = pl.BlockSpec((tm, d), lambda i: (i, 0))
    full = lambda a: pl.BlockSpec(a.shape, lambda i: (0,) * a.ndim)
    mem = pl.BlockSpec((nb, nm, mw), mem_idx)
    return pl.pallas_call(
        functools.partial(_memattn_kernel, nb=nb, tb=tb, scale=HEAD_DIM ** -0.5),
        grid=(n // tm,),
        in_specs=[row, full(gn), full(wq), full(qg), mem, mem, full(wo)],
        out_specs=row,
        out_shape=jax.ShapeDtypeStruct((n, d), F32),
        scratch_shapes=[pltpu.VMEM((tm, mw), BF16), pltpu.VMEM((tm, mw), BF16)],
        compiler_params=_cparams("parallel"),
        name="memattn",
    )(x, gn, wq, qg, mem_k, mem_v, wo)


def _layer(x, p, attend, conv_buf, h0, mem_k, mem_v):
    b, t, d = x.shape
    n = b * t
    w = h0.shape[-1]
    x1 = _ffn(x.reshape(n, d), p["ffn1_norm"], p["ffn1_wg"], p["ffn1_wu"], p["ffn1_wd"])
    q, proj, xn1 = _inproj(x1, p["mix_norm"], p["w_in"], p["q_norm"], p["k_norm"], w=w)
    a_out = attend(q, proj)
    proj3 = proj.reshape(b, t, -1)
    r_out, h_last, new_buf = _lru(proj3, conv_buf, h0, p["conv_w"], p["conv_b"], p["w_gate"],
                                  p["lru_ba"], p["lru_bx"], p["lru_lambda"])
    x2 = _merge(x1, xn1, a_out, r_out.reshape(n, w), p["w_in"], p["w_proj_a"],
                p["w_proj_b"], p["w_out"])
    x3 = _memattn(x2, p["mem_norm_x"], p["w_mq"], p["mq_norm"], mem_k, mem_v, p["w_mo"], batch=b)
    y = _ffn(x3, p["ffn2_norm"], p["ffn2_wg"], p["ffn2_wu"], p["ffn2_wd"], p["final_norm"])
    group = lambda g: proj3[..., g * w:(g + 1) * w]
    return y.reshape(b, t, d), group(PROJ_K), group(PROJ_V), new_buf, h_last


def kernel(x_prompt, x_sample, cache_attn_k, cache_attn_v, state_conv, state_lru, cache_mem_k, cache_mem_v, mem_prompt, ffn1_norm, ffn1_wg, ffn1_wu, ffn1_wd, mix_norm, w_in, q_norm, k_norm, rel_bias, conv_w, conv_b, lru_wa, lru_ba, lru_wx, lru_bx, lru_lambda, w_proj_a, w_proj_b, w_out, mem_norm_x, mem_norm_m, w_mq, w_mk, w_mv, mq_norm, mk_norm, w_mo, ffn2_norm, ffn2_wg, ffn2_wu, ffn2_wd, final_norm):
    depth = ffn1_norm.shape[0]
    bp, sp, d = x_prompt.shape
    bd, td, _ = x_sample.shape
    heads = rel_bias.shape[1]
    w = heads * HEAD_DIM
    mheads = w_mq.shape[-1] // HEAD_DIM
    n_mem = mem_prompt.shape[1]
    n_keep = min(BAND_ROWS, sp)
    yp, ys = x_prompt, x_sample
    outs = [[] for _ in range(10)]
    for l in range(depth):
        bf = lambda a: a[l].astype(BF16)
        half = lambda a: (0.5 * a[l]).astype(BF16)
        row = lambda a: a[l].reshape(1, -1)
        p = {
            "ffn1_norm": row(ffn1_norm), "ffn1_wg": bf(ffn1_wg), "ffn1_wu": bf(ffn1_wu), "ffn1_wd": half(ffn1_wd),
            "mix_norm": row(mix_norm), "w_in": bf(w_in), "q_norm": row(q_norm), "k_norm": row(k_norm),
            "conv_w": conv_w[l], "conv_b": row(conv_b),
            "w_gate": _pair_gate_weights(lru_wa[l], lru_wx[l]),
            "lru_ba": row(lru_ba), "lru_bx": row(lru_bx), "lru_lambda": row(lru_lambda),
            "w_proj_a": bf(w_proj_a), "w_proj_b": bf(w_proj_b), "w_out": bf(w_out),
            "mem_norm_x": row(mem_norm_x), "w_mq": bf(w_mq), "mq_norm": row(mq_norm), "w_mo": bf(w_mo),
            "ffn2_norm": row(ffn2_norm), "ffn2_wg": bf(ffn2_wg), "ffn2_wu": bf(ffn2_wu), "ffn2_wd": half(ffn2_wd),
            "final_norm": row(final_norm),
        }
        rb = rel_bias[l]
        mk, mv = _memkv(mem_prompt.reshape(bp * n_mem, d), row(mem_norm_m), bf(w_mk), bf(w_mv), row(mk_norm))
        mk, mv = mk.reshape(bp, n_mem, -1), mv.reshape(bp, n_mem, -1)
        attend_p = lambda q, proj: _band_attn(q, proj, rb, batch=bp)
        yp, k_p, v_p, cb_p, hl_p = _layer(
            yp, p, attend_p, jnp.zeros((bp, CONV_W - 1, w), F32), jnp.zeros((bp, w), F32), mk, mv)
        heads_of = lambda a: a.reshape(a.shape[:2] + (heads, HEAD_DIM))
        keep = lambda a: heads_of(a[:, sp - n_keep:])
        attend_s = lambda q, proj: _step_attn(q, proj[:, PROJ_K * w:(PROJ_K + 1) * w],
                                              proj[:, PROJ_V * w:(PROJ_V + 1) * w], rb,
                                              cache_attn_k[l], cache_attn_v[l])
        ys, k_s, v_s, cb_s, hl_s = _layer(
            ys, p, attend_s, state_conv[l], state_lru[l],
            cache_mem_k[l].reshape(bd, n_mem, -1), cache_mem_v[l].reshape(bd, n_mem, -1))
        new = [keep(k_p), keep(v_p), cb_p, hl_p,
               mk.reshape(bp, n_mem, mheads, HEAD_DIM), mv.reshape(bp, n_mem, mheads, HEAD_DIM),
               heads_of(k_s), heads_of(v_s), cb_s, hl_s]
        for o, a in zip(outs, new):
            o.append(a)
    return (yp, ys) + tuple(jnp.stack(o) for o in outs)
```

```python
import functools

import numpy as np
import jax
import jax.numpy as jnp
from jax import lax
from jax.experimental import pallas as pl
from jax.experimental.pallas import tpu as pltpu

F32 = jnp.float32
BF16 = jnp.bfloat16

CHUNK = 64
LEFT_CHUNKS = 8
BAND_ROWS = LEFT_CHUNKS * CHUNK
HEAD_DIM = 128
PAST_LEN = 1024
CONV_W = 4
LRU_C = 8.0
EPS = 1e-6
NEG_INF = -1e30

VMEM_LIMIT_BYTES = 56 * 1024 * 1024
LANES = 128
SUBLANES = 8

FFN_ROW_TILE = 1024
INPROJ_ROW_TILE = 1024
ROW_TILE = 512
FFN_COL_TILE = 512
FFN_NORM_BLOCKS = 4
MERGE_COL_TILE = 512
ATTN_Q_TILE = 256
LRU_TIME_TILE = 128
STEP_ATTN_SEQS = 2
MEMATTN_ROW_TILE = 1024
MEM_ROW_TILE = 128


def _cparams(*sem):
    return pltpu.CompilerParams(dimension_semantics=sem, vmem_limit_bytes=VMEM_LIMIT_BYTES)


def _rms(x, g):
    ms = jnp.mean(x * x, axis=-1, keepdims=True)
    return x * lax.rsqrt(ms + EPS) * g


def _store_head_rms(u, g, o_ref):
    for h in range(u.shape[-1] // HEAD_DIM):
        sl = slice(h * HEAD_DIM, (h + 1) * HEAD_DIM)
        o_ref[:, sl] = _rms(u[:, sl], g).astype(o_ref.dtype)


def _sigmoid(x):
    return 0.5 * jnp.tanh(0.5 * x) + 0.5


def _dot(a, b):
    return jnp.dot(a, b, preferred_element_type=F32)


def _dot_nt(a, b):
    return lax.dot_general(a, b, (((1,), (1,)), ((), ())), preferred_element_type=F32)


def _ffn_kernel(*refs, nf, final_norm):
    if final_norm:
        x_ref, gn_ref, wg_ref, wu_ref, wd_ref, gf_ref, o_ref, xn_ref = refs
    else:
        x_ref, gn_ref, wg_ref, wu_ref, wd_ref, o_ref, xn_ref = refs
    f = pl.program_id(1)

    def chunk(xn):
        h = (jax.nn.silu(_dot(xn, wg_ref[...])) * _dot(xn, wu_ref[...])).astype(BF16)
        return _dot(h, wd_ref[...])

    @pl.when(f == 0)
    def _():
        rb = x_ref.shape[0] // FFN_NORM_BLOCKS
        for r in range(0, x_ref.shape[0], rb):
            x = x_ref[r:r + rb, :]
            xn = _rms(x, gn_ref[...]).astype(BF16)
            xn_ref[r:r + rb, :] = xn
            o_ref[r:r + rb, :] = x + chunk(xn)

    @pl.when(f > 0)
    def _():
        o_ref[...] += chunk(xn_ref[...])

    if final_norm:
        @pl.when(f == nf - 1)
        def _():
            o_ref[...] = _rms(o_ref[...], gf_ref[...])


def _ffn(x, gn, wg, wu, wd, final_gain=None):
    n, d = x.shape
    dff = wg.shape[1]
    tm, tf = min(FFN_ROW_TILE, n), FFN_COL_TILE
    nf = dff // tf
    final_norm = final_gain is not None
    row = pl.BlockSpec((tm, d), lambda i, f: (i, 0))
    vec = pl.BlockSpec((1, d), lambda i, f: (0, 0))
    in_specs = [row, vec,
                pl.BlockSpec((d, tf), lambda i, f: (0, f)),
                pl.BlockSpec((d, tf), lambda i, f: (0, f)),
                pl.BlockSpec((tf, d), lambda i, f: (f, 0))]
    args = [x, gn, wg, wu, wd]
    if final_norm:
        in_specs.append(vec)
        args.append(final_gain)
    return pl.pallas_call(
        functools.partial(_ffn_kernel, nf=nf, final_norm=final_norm),
        grid=(n // tm, nf),
        in_specs=in_specs,
        out_specs=row,
        out_shape=jax.ShapeDtypeStruct((n, d), F32),
        scratch_shapes=[pltpu.VMEM((tm, d), BF16)],
        compiler_params=_cparams("parallel", "arbitrary"),
        name="ffn",
    )(*args)


PROJ_K, PROJ_V, PROJ_XR, PROJ_GR = range(4)
N_PROJ_GROUPS = 5


def _inproj_kernel(x_ref, gn_ref, w_ref, qg_ref, kg_ref, q_ref, o_ref, xn_ref):
    j = pl.program_id(1)

    @pl.when(j == 0)
    def _():
        xn_ref[...] = _rms(x_ref[...], gn_ref[...]).astype(BF16)

    def normed(out_ref, g_ref):
        step = 2 * HEAD_DIM
        for c in range(0, w_ref.shape[1], step):
            u = _dot(xn_ref[...], w_ref[:, c:c + step])
            for h in range(c, c + step, HEAD_DIM):
                out_ref[:, h:h + HEAD_DIM] = _rms(u[:, h - c:h - c + HEAD_DIM], g_ref[...]).astype(out_ref.dtype)

    def plain():
        o_ref[...] = _dot(xn_ref[...], w_ref[...])

    pl.when(j == 0)(functools.partial(normed, q_ref, qg_ref))
    pl.when(j == 1)(functools.partial(normed, o_ref, kg_ref))
    pl.when(j >= 2)(plain)


def _inproj(x, gn, w_in, qg, kg, *, w):
    n, d = x.shape
    tm = min(INPROJ_ROW_TILE, n)
    row = pl.BlockSpec((tm, d), lambda i, j: (i, 0))
    vec = lambda m: pl.BlockSpec((1, m), lambda i, j: (0, 0))
    return pl.pallas_call(
        _inproj_kernel,
        grid=(n // tm, N_PROJ_GROUPS),
        in_specs=[row, vec(d), pl.BlockSpec((d, w), lambda i, j: (0, j)),
                  vec(HEAD_DIM), vec(HEAD_DIM)],
        out_specs=[pl.BlockSpec((tm, w), lambda i, j: (i, 0)),
                   pl.BlockSpec((tm, w), lambda i, j: (i, jnp.maximum(j - 1, 0))),
                   row],
        out_shape=[jax.ShapeDtypeStruct((n, w), BF16),
                   jax.ShapeDtypeStruct((n, (N_PROJ_GROUPS - 1) * w), F32),
                   jax.ShapeDtypeStruct((n, d), BF16)],
        compiler_params=_cparams("parallel", "arbitrary"),
        name="inproj",
    )(x, gn, w_in, qg, kg)


def _memkv_kernel(m_ref, gn_ref, wk_ref, wv_ref, kg_ref, k_ref, v_ref):
    xn = _rms(m_ref[...], gn_ref[...]).astype(BF16)
    _store_head_rms(_dot(xn, wk_ref[...]), kg_ref[...], k_ref)
    v_ref[...] = _dot(xn, wv_ref[...])


def _memkv(m, gn, wk, wv, kg):
    n, d = m.shape
    mw = wk.shape[1]
    tm = min(ROW_TILE, n)
    row = lambda c: pl.BlockSpec((tm, c), lambda i: (i, 0))
    full = lambda a: pl.BlockSpec(a.shape, lambda i: (0, 0))
    return pl.pallas_call(
        _memkv_kernel,
        grid=(n // tm,),
        in_specs=[row(d), full(gn), full(wk), full(wv), full(kg)],
        out_specs=[row(mw), row(mw)],
        out_shape=[jax.ShapeDtypeStruct((n, mw), F32)] * 2,
        compiler_params=_cparams("parallel"),
        name="memkv",
    )(m, gn, wk, wv, kg)


def _band_attn_kernel(q_ref, k_ref, v_ref, g_ref, vis_ref, o_ref, bias_s, kb_s, vb_s,
                      *, tq, win, nt, scale):
    nvar = vis_ref.shape[0]

    @pl.when(pl.program_id(1) == 0)
    def _():
        for var in range(nvar):
            row = jnp.broadcast_to(g_ref[0, var:var + 1, :], (tq, g_ref.shape[-1]))
            toeplitz = pltpu.roll(row, 0, 1, stride=1, stride_axis=0)
            bias_s[var] = jnp.where(vis_ref[var] > 0.5, toeplitz[:, :win], NEG_INF)

    kb_s[...] = k_ref[...].astype(BF16)
    vb_s[...] = v_ref[...].astype(BF16)

    def tile(i, carry):
        q0 = pl.multiple_of(i * tq, tq)
        ws = pl.multiple_of(jnp.maximum(i * tq - BAND_ROWS, 0), tq)
        var = jnp.minimum(i, nvar - 1)
        s = _dot_nt(q_ref[pl.ds(q0, tq), :], kb_s[pl.ds(ws, win), :]) * scale + bias_s[var]
        m = jnp.max(s, axis=-1, keepdims=True)
        p = jnp.exp(s - m)
        l = jnp.sum(p, axis=-1, keepdims=True)
        o = _dot(p.astype(BF16), vb_s[pl.ds(ws, win), :])
        o_ref[pl.ds(q0, tq), :] = (o / l).astype(o_ref.dtype)
        return carry

    lax.fori_loop(0, nt, tile, 0, unroll=8)


def _band_tables(rel_bias, tq):
    max_rel = (rel_bias.shape[-1] - 1) // 2
    win = BAND_ROWS + tq
    span = win + tq
    nvar = BAND_ROWS // tq + 1
    off = np.arange(nvar)[:, None] * tq
    c = np.arange(span)[None, :]
    rel = np.where(c < win, off - c, off + span - c)
    gen = rel_bias.astype(F32)[:, np.clip(rel, -max_rel, max_rel) + max_rel]
    qc = (off[:, :, None] + np.arange(tq)[None, :, None]) // CHUNK
    kc = np.arange(win)[None, None, :] // CHUNK
    vis = (kc <= qc) & (kc >= qc - LEFT_CHUNKS)
    return gen, jnp.asarray(vis.astype(np.float32))


def _band_attn(q, kv, rel_bias, *, batch):
    n, w = q.shape
    seq = n // batch
    heads = w // HEAD_DIM
    tq = ATTN_Q_TILE
    win = BAND_ROWS + tq
    gen, vis = _band_tables(rel_bias, tq)
    blk = pl.BlockSpec((seq, HEAD_DIM), lambda h, b: (b, h))
    k_blk = pl.BlockSpec((seq, HEAD_DIM), lambda h, b: (b, PROJ_K * heads + h))
    v_blk = pl.BlockSpec((seq, HEAD_DIM), lambda h, b: (b, PROJ_V * heads + h))
    return pl.pallas_call(
        functools.partial(_band_attn_kernel, tq=tq, win=win, nt=seq // tq, scale=HEAD_DIM ** -0.5),
        grid=(heads, batch),
        in_specs=[blk, k_blk, v_blk,
                  pl.BlockSpec((1,) + gen.shape[1:], lambda h, b: (h, 0, 0)),
                  pl.BlockSpec(vis.shape, lambda h, b: (0, 0, 0))],
        out_specs=blk,
        out_shape=jax.ShapeDtypeStruct((n, w), BF16),
        scratch_shapes=[pltpu.VMEM(vis.shape, F32), pltpu.VMEM((seq, HEAD_DIM), BF16),
                        pltpu.VMEM((seq, HEAD_DIM), BF16)],
        compiler_params=_cparams("arbitrary", "arbitrary"),
        name="band_attn",
    )(q, kv, kv, gen, vis)


def _step_attn_kernel(q_ref, kn_ref, vn_ref, kc_ref, vc_ref, bc_ref, bn_ref, o_ref, *, scale):
    for s in range(q_ref.shape[0]):
        q = q_ref[s]
        sc = _dot_nt(q, kc_ref[s].astype(BF16)) * scale + bc_ref[...]
        sn = _dot_nt(q, kn_ref[s].astype(BF16)) * scale + bn_ref[...]
        m = jnp.maximum(jnp.max(sc, axis=-1, keepdims=True), jnp.max(sn, axis=-1, keepdims=True))
        pc = jnp.exp(sc - m)
        pn = jnp.exp(sn - m)
        l = jnp.sum(pc, axis=-1, keepdims=True) + jnp.sum(pn, axis=-1, keepdims=True)
        o = (_dot(pc.astype(BF16), vc_ref[s].astype(BF16))
             + _dot(pn.astype(BF16), vn_ref[s].astype(BF16)))
        o_ref[s] = (o / l).astype(o_ref.dtype)


def _step_attn(q, k, v, rel_bias, k_cache, v_cache):
    bd, wc, heads, hd = k_cache.shape
    t = q.shape[0] // bd
    max_rel = (rel_bias.shape[-1] - 1) // 2
    qpos = PAST_LEN + np.arange(t)
    kpos = np.concatenate([PAST_LEN - wc + np.arange(wc), PAST_LEN + np.arange(t)])
    idx = np.clip(qpos[:, None] - kpos[None, :], -max_rel, max_rel) + max_rel
    qc, kc = qpos[:, None] // CHUNK, kpos[None, :] // CHUNK
    vis = (kc <= qc) & (kc >= qc - LEFT_CHUNKS) & (kpos[None, :] >= 0)
    bias = jnp.where(vis[None], rel_bias.astype(F32)[:, idx], NEG_INF)
    same_head = np.eye(heads, dtype=bool)[None, :, None, :]
    bias = jnp.where(same_head, bias.transpose(1, 0, 2)[:, :, :, None], NEG_INF)
    bias_c = bias[:, :, :wc].reshape(t * heads, wc * heads)
    bias_n = bias[:, :, wc:].reshape(t * heads, t * heads)
    pairs = lambda a: a.reshape(bd, t * heads, hd)
    ns = STEP_ATTN_SEQS if bd % STEP_ATTN_SEQS == 0 else 1
    new = pl.BlockSpec((ns, t * heads, hd), lambda b: (b, 0, 0))
    cache = pl.BlockSpec((ns, wc * heads, hd), lambda b: (b, 0, 0))
    full = lambda a: pl.BlockSpec(a.shape, lambda b: (0,) * a.ndim)
    out = pl.pallas_call(
        functools.partial(_step_attn_kernel, scale=HEAD_DIM ** -0.5),
        grid=(bd // ns,),
        in_specs=[new, new, new, cache, cache, full(bias_c), full(bias_n)],
        out_specs=new,
        out_shape=jax.ShapeDtypeStruct((bd, t * heads, hd), BF16),
        compiler_params=_cparams("parallel"),
        name="step_attn",
    )(pairs(q), pairs(k), pairs(v), k_cache.reshape(bd, wc * heads, hd),
      v_cache.reshape(bd, wc * heads, hd), bias_c, bias_n)
    return out.reshape(q.shape)


def _lru_kernel(xr_ref, gr_ref, buf0_ref, h0_ref, cw_ref, cb_ref, wgate_ref, ba_ref, bx_ref,
                lam_ref, ro_ref, hl_ref, tail_ref, xp_s, xc_s, a_s, b_s, h_s, *, tt, nb, pitch):
    nslab = a_s.shape[0]
    lane = lambda p: slice(p * LANES, (p + 1) * LANES)
    hist0 = SUBLANES - (CONV_W - 1)

    @pl.when(pl.program_id(0) == 0)
    def _():
        xp_s[:, 0:SUBLANES, :] = buf0_ref[...]
        xc_s[...] = jnp.zeros_like(xc_s)
        for p in range(nslab):
            h_s[p] = h0_ref[:, lane(p)]

    def conv(b, carry):
        xp_s[b, SUBLANES:SUBLANES + tt, :] = xr_ref[b]
        xc = cb_ref[...]
        for j in range(CONV_W):
            xc = xc + xp_s[b, hist0 + j:hist0 + j + tt, :] * cw_ref[j:j + 1, :]
        xc_s[pl.ds(pl.multiple_of(b * pitch, SUBLANES), tt), :] = xc
        xp_s[b, 0:SUBLANES, :] = xp_s[b, tt:tt + SUBLANES, :]
        return carry

    lax.fori_loop(0, nb, conv, 0)
    tail_ref[...] = xp_s[:, 0:SUBLANES, :]

    sp = jax.nn.softplus(-lam_ref[...])
    for p in range(nslab):
        xc = xc_s[:, lane(p)]
        z = _dot(xc.astype(BF16), wgate_ref[p])
        r = _sigmoid(z[:, :LANES] + ba_ref[:, lane(p)])
        ig = _sigmoid(z[:, LANES:] + bx_ref[:, lane(p)])
        log_a = -LRU_C * r * sp[:, lane(p)]
        a = jnp.exp(log_a)
        a_s[p] = a
        m2 = -jnp.tanh(log_a) * (a * a + 1.0)
        mult = jnp.where(m2 > 0.0, m2 * lax.rsqrt(m2), 0.0)
        b_s[p] = mult * (ig * xc)

    def step(t, hs):
        rows = pl.ds(t, nb, stride=pitch)
        new = []
        for p in range(nslab):
            h = a_s[p, rows, :] * hs[p] + b_s[p, rows, :]
            b_s[p, rows, :] = h
            new.append(h)
        return tuple(new)

    hs = lax.fori_loop(0, tt, step, tuple(h_s[p] for p in range(nslab)))
    for p in range(nslab):
        h_s[p] = hs[p]
        hl_ref[:, lane(p)] = hs[p]

    def gate_out(b, carry):
        r0 = pl.multiple_of(b * pitch, SUBLANES)
        for p in range(nslab):
            ro_ref[b, :, lane(p)] = (jax.nn.gelu(gr_ref[b, :, lane(p)])
                                     * b_s[p, pl.ds(r0, tt), :]).astype(ro_ref.dtype)
        return carry

    lax.fori_loop(0, nb, gate_out, 0)


def _lru(proj, conv_buf, h0, conv_w, conv_b, wgate, ba, bx, lam):
    nb, t, _ = proj.shape
    r = h0.shape[-1]
    tt = min(LRU_TIME_TILE, t)
    pitch = tt + SUBLANES
    nslab = r // LANES
    buf0 = jnp.pad(conv_buf, ((0, 0), (SUBLANES - (CONV_W - 1), 0), (0, 0)))
    blk = pl.BlockSpec((nb, tt, r), lambda i: (0, i, 0))
    xr_blk = pl.BlockSpec((nb, tt, r), lambda i: (0, i, PROJ_XR))
    gr_blk = pl.BlockSpec((nb, tt, r), lambda i: (0, i, PROJ_GR))
    full = lambda a: pl.BlockSpec(a.shape, lambda i: (0,) * a.ndim)
    r_out, h_last, tail = pl.pallas_call(
        functools.partial(_lru_kernel, tt=tt, nb=nb, pitch=pitch),
        grid=(t // tt,),
        in_specs=[xr_blk, gr_blk, full(buf0), full(h0), full(conv_w), full(conv_b), full(wgate),
                  full(ba), full(bx), full(lam)],
        out_specs=[blk, full(h0), full(buf0)],
        out_shape=[jax.ShapeDtypeStruct((nb, t, r), BF16),
                   jax.ShapeDtypeStruct(h0.shape, F32),
                   jax.ShapeDtypeStruct(buf0.shape, F32)],
        scratch_shapes=[pltpu.VMEM((nb, tt + SUBLANES, r), F32),
                        pltpu.VMEM((nb * pitch, r), F32),
                        pltpu.VMEM((nslab, nb * pitch, LANES), F32),
                        pltpu.VMEM((nslab, nb * pitch, LANES), F32),
                        pltpu.VMEM((nslab, nb, LANES), F32)],
        compiler_params=_cparams("arbitrary"),
        name="conv_lru",
    )(proj, proj, buf0, h0, conv_w, conv_b, wgate, ba, bx, lam)
    return r_out, h_last, tail[:, SUBLANES - (CONV_W - 1):]


def _pair_gate_weights(wa, wx):
    nblk, bs, _ = wa.shape
    def pair(w):
        w = w.reshape(nblk // 2, 2, bs, bs)
        z = jnp.zeros_like(w[:, 0])
        top = jnp.concatenate([w[:, 0], z], axis=-1)
        bot = jnp.concatenate([z, w[:, 1]], axis=-1)
        return jnp.concatenate([top, bot], axis=-2)
    return jnp.concatenate([pair(wa), pair(wx)], axis=-1).astype(BF16)


def _merge_kernel(x_ref, xn_ref, a_ref, r_ref, wga_ref, wgb_ref, wpa_ref, wpb_ref, wo_ref, o_ref):
    @pl.when(pl.program_id(1) == 0)
    def _():
        o_ref[...] = x_ref[...]

    xn = xn_ref[...]
    m = (jax.nn.sigmoid(_dot(xn, wga_ref[...])) * _dot(a_ref[...], wpa_ref[...])
         + jax.nn.sigmoid(_dot(xn, wgb_ref[...])) * _dot(r_ref[...], wpb_ref[...]))
    o_ref[...] += _dot(m.astype(BF16), wo_ref[...])


def _merge(x, xn, a_out, r_out, w_in, w_pa, w_pb, w_out):
    n, d = x.shape
    w = a_out.shape[1]
    tm, tc = min(ROW_TILE, n), MERGE_COL_TILE
    nc = d // tc
    ga0 = (w_in.shape[1] - 2 * d) // tc
    gb0 = ga0 + nc
    row = pl.BlockSpec((tm, d), lambda i, c: (i, 0))
    a_spec = pl.BlockSpec((tm, w), lambda i, c: (i, 0))
    return pl.pallas_call(
        _merge_kernel,
        grid=(n // tm, nc),
        in_specs=[row, row, a_spec, a_spec,
                  pl.BlockSpec((d, tc), lambda i, c: (0, ga0 + c)),
                  pl.BlockSpec((d, tc), lambda i, c: (0, gb0 + c)),
                  pl.BlockSpec((w, tc), lambda i, c: (0, c)),
                  pl.BlockSpec((w, tc), lambda i, c: (0, c)),
                  pl.BlockSpec((tc, d), lambda i, c: (c, 0))],
        out_specs=row,
        out_shape=jax.ShapeDtypeStruct((n, d), F32),
        compiler_params=_cparams("parallel", "arbitrary"),
        name="merge",
    )(x, xn, a_out, r_out, w_in, w_in, w_pa, w_pb, w_out)


def _memattn_kernel(x_ref, gn_ref, wq_ref, qg_ref, mk_ref, mv_ref, wo_ref, o_ref, mq_s, att_s,
                    *, nb, tb, scale):
    x = x_ref[...]
    _store_head_rms(_dot(_rms(x, gn_ref[...]).astype(BF16), wq_ref[...]), qg_ref[...], mq_s)
    for h in range(mq_s.shape[-1] // HEAD_DIM):
        sl = slice(h * HEAD_DIM, (h + 1) * HEAD_DIM)
        for jb in range(nb):
            rs = slice(jb * tb, (jb + 1) * tb)
            s = _dot_nt(mq_s[rs, sl], mk_ref[jb, :, sl].astype(BF16)) * scale
            p = jnp.exp(s - jnp.max(s, axis=-1, keepdims=True))
            l = jnp.sum(p, axis=-1, keepdims=True)
            o = _dot(p.astype(BF16), mv_ref[jb, :, sl].astype(BF16))
            att_s[rs, sl] = (o / l).astype(BF16)
    o_ref[...] = x + _dot(att_s[...], wo_ref[...])


def _memattn(x, gn, wq, qg, mem_k, mem_v, wo, *, batch):
    n, d = x.shape
    bm, nm, mw = mem_k.shape
    seq = n // batch
    if seq >= MEMATTN_ROW_TILE:
        tm, nb, tb = MEMATTN_ROW_TILE, 1, MEMATTN_ROW_TILE
        per = seq // tm
        mem_idx = lambda i: (i // per, 0, 0)
    else:
        tm = MEM_ROW_TILE
        nb, tb = tm // seq, seq
        mem_idx = lambda i: (i, 0, 0)
    row = pl.BlockSpec((tm, d), lambda i: (i, 0))
    full = lambda a: pl.BlockSpec(a.shape, lambda i: (0,) * a.ndim)
    mem = pl.BlockSpec((nb, nm, mw), mem_idx)
    return pl.pallas_call(
        functools.partial(_memattn_kernel, nb=nb, tb=tb, scale=HEAD_DIM ** -0.5),
        grid=(n // tm,),
        in_specs=[row, full(gn), full(wq), full(qg), mem, mem, full(wo)],
        out_specs=row,
        out_shape=jax.ShapeDtypeStruct((n, d), F32),
        scratch_shapes=[pltpu.VMEM((tm, mw), BF16), pltpu.VMEM((tm, mw), BF16)],
        compiler_params=_cparams("parallel"),
        name="memattn",
    )(x, gn, wq, qg, mem_k, mem_v, wo)


def _layer(x, p, attend, conv_buf, h0, mem_k, mem_v):
    b, t, d = x.shape
    n = b * t
    w = h0.shape[-1]
    x1 = _ffn(x.reshape(n, d), p["ffn1_norm"], p["ffn1_wg"], p["ffn1_wu"], p["ffn1_wd"])
    q, proj, xn1 = _inproj(x1, p["mix_norm"], p["w_in"], p["q_norm"], p["k_norm"], w=w)
    a_out = attend(q, proj)
    proj3 = proj.reshape(b, t, -1)
    r_out, h_last, new_buf = _lru(proj3, conv_buf, h0, p["conv_w"], p["conv_b"], p["w_gate"],
                                  p["lru_ba"], p["lru_bx"], p["lru_lambda"])
    x2 = _merge(x1, xn1, a_out, r_out.reshape(n, w), p["w_in"], p["w_proj_a"],
                p["w_proj_b"], p["w_out"])
    x3 = _memattn(x2, p["mem_norm_x"], p["w_mq"], p["mq_norm"], mem_k, mem_v, p["w_mo"], batch=b)
    y = _ffn(x3, p["ffn2_norm"], p["ffn2_wg"], p["ffn2_wu"], p["ffn2_wd"], p["final_norm"])
    group = lambda g: proj3[..., g * w:(g + 1) * w]
    return y.reshape(b, t, d), group(PROJ_K), group(PROJ_V), new_buf, h_last


def kernel(x_prompt, x_sample, cache_attn_k, cache_attn_v, state_conv, state_lru, cache_mem_k, cache_mem_v, mem_prompt, ffn1_norm, ffn1_wg, ffn1_wu, ffn1_wd, mix_norm, w_in, q_norm, k_norm, rel_bias, conv_w, conv_b, lru_wa, lru_ba, lru_wx, lru_bx, lru_lambda, w_proj_a, w_proj_b, w_out, mem_norm_x, mem_norm_m, w_mq, w_mk, w_mv, mq_norm, mk_norm, w_mo, ffn2_norm, ffn2_wg, ffn2_wu, ffn2_wd, final_norm):
    depth = ffn1_norm.shape[0]
    bp, sp, d = x_prompt.shape
    bd, td, _ = x_sample.shape
    heads = rel_bias.shape[1]
    w = heads * HEAD_DIM
    mheads = w_mq.shape[-1] // HEAD_DIM
    n_mem = mem_prompt.shape[1]
    n_keep = min(BAND_ROWS, sp)
    yp, ys = x_prompt, x_sample
    outs = [[] for _ in range(10)]
    for l in range(depth):
        bf = lambda a: a[l].astype(BF16)
        half = lambda a: (0.5 * a[l]).astype(BF16)
        row = lambda a: a[l].reshape(1, -1)
        p = {
            "ffn1_norm": row(ffn1_norm), "ffn1_wg": bf(ffn1_wg), "ffn1_wu": bf(ffn1_wu), "ffn1_wd": half(ffn1_wd),
            "mix_norm": row(mix_norm), "w_in": bf(w_in), "q_norm": row(q_norm), "k_norm": row(k_norm),
            "conv_w": conv_w[l], "conv_b": row(conv_b),
            "w_gate": _pair_gate_weights(lru_wa[l], lru_wx[l]),
            "lru_ba": row(lru_ba), "lru_bx": row(lru_bx), "lru_lambda": row(lru_lambda),
            "w_proj_a": bf(w_proj_a), "w_proj_b": bf(w_proj_b), "w_out": bf(w_out),
            "mem_norm_x": row(mem_norm_x), "w_mq": bf(w_mq), "mq_norm": row(mq_norm), "w_mo": bf(w_mo),
            "ffn2_norm": row(ffn2_norm), "ffn2_wg": bf(ffn2_wg), "ffn2_wu": bf(ffn2_wu), "ffn2_wd": half(ffn2_wd),
            "final_norm": row(final_norm),
        }
        rb = rel_bias[l]
        mk, mv = _memkv(mem_prompt.reshape(bp * n_mem, d), row(mem_norm_m), bf(w_mk), bf(w_mv), row(mk_norm))
        mk, mv = mk.reshape(bp, n_mem, -1), mv.reshape(bp, n_mem, -1)
        attend_p = lambda q, proj: _band_attn(q, proj, rb, batch=bp)
        yp, k_p, v_p, cb_p, hl_p = _layer(
            yp, p, attend_p, jnp.zeros((bp, CONV_W - 1, w), F32), jnp.zeros((bp, w), F32), mk, mv)
        heads_of = lambda a: a.reshape(a.shape[:2] + (heads, HEAD_DIM))
        keep = lambda a: heads_of(a[:, sp - n_keep:])
        attend_s = lambda q, proj: _step_attn(q, proj[:, PROJ_K * w:(PROJ_K + 1) * w],
                                              proj[:, PROJ_V * w:(PROJ_V + 1) * w], rb,
                                              cache_attn_k[l], cache_attn_v[l])
        ys, k_s, v_s, cb_s, hl_s = _layer(
            ys, p, attend_s, state_conv[l], state_lru[l],
            cache_mem_k[l].reshape(bd, n_mem, -1), cache_mem_v[l].reshape(bd, n_mem, -1))
        new = [keep(k_p), keep(v_p), cb_p, hl_p,
               mk.reshape(bp, n_mem, mheads, HEAD_DIM), mv.reshape(bp, n_mem, mheads, HEAD_DIM),
               heads_of(k_s), heads_of(v_s), cb_s, hl_s]
        for o, a in zip(outs, new):
            o.append(a)
    return (yp, ys) + tuple(jnp.stack(o) for o in outs)
```
